```python
import math
import jax, jax.numpy as jnp
from jax import lax
import numpy as np

D_MODEL = 1024
BATCH = 8
SEQ = 8192
DEPTH = 1

CHUNK = 64
N_HEADS_A = 8
HEAD_DIM_A = 64
ATTN_WIDTH = 512
KV_RANK = 128
IDX_HEADS = 4
IDX_DIM = 64
IDX_SCALE = 0.0625
TOPK_MAX = 256
Q_BLOCK = 128
N_BUCKETS = 32
MAX_DISTANCE = 128
S5_WIDTH = 512
S5_GROUP = 16
S5_GROUPS = 32
S5_STATE = 64
N_EXPERT_GROUPS = 4
EXPERTS_PER_GROUP = 8
N_EXPERTS = 32
D_EXPERT = 512
MOE_BLOCK = 128
EPS = 1e-6
SPLIT_POINTS = (512, 640, 896, 960, 964, 1476)
D_IN = 3524

kernel_name = "hybrid_dsa_s5_hiermoe_block"


def rmsnorm(x, g):
    xf = x.astype(jnp.float32)
    y = xf * lax.rsqrt(jnp.mean(xf * xf, axis=-1, keepdims=True) + EPS)
    return (y * g.astype(jnp.float32)).astype(x.dtype)


def t5_bucket(rel):
    half = N_BUCKETS // 2
    max_exact = half // 2
    ret = jnp.where(rel > 0, half, 0)
    n = jnp.abs(rel)
    nf = jnp.maximum(n, 1).astype(jnp.float32)
    large = max_exact + (jnp.log(nf / max_exact) / math.log(MAX_DISTANCE / max_exact)
                         * (half - max_exact)).astype(jnp.int32)
    large = jnp.minimum(large, half - 1)
    return ret + jnp.where(n < max_exact, n, large)


def dsa_attention(q, c, q_idx, k_idx, w_idx, w_uk, w_uv, rel_bias):
    B, S = q.shape[0], q.shape[1]
    n_blk = S // Q_BLOCK
    top_k = min(TOPK_MAX, S // 4)
    q_lat = jnp.einsum('bshd,rhd->bshr', q, w_uk) * (HEAD_DIM_A ** -0.5)

    def to_blocks(a):
        a = a.reshape((B, n_blk, Q_BLOCK) + a.shape[2:])
        return jnp.moveaxis(a, 1, 0)

    key_pos = jnp.arange(S, dtype=jnp.int32)

    def one_block(args):
        j, ql, qi, wi = args
        t = j * Q_BLOCK + jnp.arange(Q_BLOCK, dtype=jnp.int32)
        limit = (t // CHUNK + 1) * CHUNK
        act = jax.nn.relu(jnp.einsum('bqhd,bsd->bqhs', qi, k_idx))
        score = jnp.einsum('bqh,bqhs->bqs', wi, act).astype(jnp.float32)
        adm = key_pos[None, :] < limit[:, None]
        score = jnp.where(adm[None], score, -jnp.inf)
        _, idx = lax.top_k(score, top_k)
        valid = idx < limit[None, :, None]
        c_sel = jax.vmap(lambda cb, ib: cb[ib])(c, idx)
        logits = jnp.einsum('bqhr,bqkr->bqhk', ql, c_sel).astype(jnp.float32)
        bias = rel_bias[t5_bucket(idx - t[None, :, None])]
        logits = logits + jnp.moveaxis(bias, -1, 2).astype(jnp.float32)
        logits = jnp.where(valid[:, :, None, :], logits, -jnp.inf)
        p = jax.nn.softmax(logits, axis=-1).astype(c.dtype)
        return jnp.einsum('bqhk,bqkr->bqhr', p, c_sel)

    o_lat = lax.map(one_block, (jnp.arange(n_blk, dtype=jnp.int32), to_blocks(q_lat),
                                to_blocks(q_idx), to_blocks(w_idx)))
    o_lat = jnp.moveaxis(o_lat, 0, 1).reshape(B, S, N_HEADS_A, KV_RANK)
    o = jnp.einsum('bshr,rhd->bshd', o_lat, w_uv)
    return o.reshape(B, S, ATTN_WIDTH)


def _complex_affine_combine(e1, e2):
    a1r, a1i, b1r, b1i = e1
    a2r, a2i, b2r, b2i = e2
    ar = a2r * a1r - a2i * a1i
    ai = a2r * a1i + a2i * a1r
    br = a2r * b1r - a2i * b1i + b2r
    bi = a2r * b1i + a2i * b1r + b2i
    return (ar, ai, br, bi)


def s5_mixer(u, a_re, a_im, log_dt, b_re, b_im, c_re, c_im, d_skip, w_glu):
    B, S = u.shape[0], u.shape[1]
    f32 = jnp.float32
    uf = u.reshape(B, S, S5_GROUPS, S5_GROUP).astype(f32)
    dt = jnp.exp(log_dt.astype(f32))[:, None]
    ar, ai = a_re.astype(f32), a_im.astype(f32)
    mag = jnp.exp(ar * dt)
    abar_r = mag * jnp.cos(ai * dt)
    abar_i = mag * jnp.sin(ai * dt)
    nr, ni = abar_r - 1.0, abar_i
    den = ar * ar + ai * ai
    fr = (nr * ar + ni * ai) / den
    fi = (ni * ar - nr * ai) / den
    br_, bi_ = b_re.astype(f32), b_im.astype(f32)
    bb_r = fr[..., None] * br_ - fi[..., None] * bi_
    bb_i = fr[..., None] * bi_ + fi[..., None] * br_
    bu_r = jnp.einsum('bsgp,gnp->bsgn', uf, bb_r)
    bu_i = jnp.einsum('bsgp,gnp->bsgn', uf, bb_i)
    shape_a = (1, S, S5_GROUPS, S5_STATE)
    elems = (jnp.broadcast_to(abar_r, shape_a), jnp.broadcast_to(abar_i, shape_a), bu_r, bu_i)
    _, _, x_r, x_i = lax.associative_scan(_complex_affine_combine, elems, axis=1)
    y = (jnp.einsum('bsgn,gpn->bsgp', x_r, c_re.astype(f32))
         - jnp.einsum('bsgn,gpn->bsgp', x_i, c_im.astype(f32))
         + d_skip.astype(f32) * uf)
    z = jax.nn.gelu(y.reshape(B, S, S5_WIDTH).astype(u.dtype))
    val, gate = jnp.split(z @ w_glu, 2, axis=-1)
    return val * jax.nn.sigmoid(gate)


def hier_moe(h, w_rg, b_rg, w_re, b_rexp, w_gate, w_up, w_down):
    B, S, D = h.shape
    T = B * S
    f32 = jnp.float32
    xf = h.reshape(T, D)
    pg = jax.nn.softmax((xf @ w_rg).astype(f32) + b_rg.astype(f32), axis=-1)
    grp = jnp.argmax(pg, axis=-1).astype(jnp.int32)
    gval = jnp.max(pg, axis=-1)
    le = ((xf @ w_re).astype(f32) + b_rexp.astype(f32)).reshape(T, N_EXPERT_GROUPS, EXPERTS_PER_GROUP)
    le_sel = le[jnp.arange(T), grp]
    top_p, top_i = lax.top_k(jax.nn.softmax(le_sel, axis=-1), 2)
    top_p = top_p / jnp.sum(top_p, axis=-1, keepdims=True)
    weight = gval[:, None] * top_p
    eid = grp[:, None] * EXPERTS_PER_GROUP + top_i.astype(jnp.int32)

    M = 2 * T
    e_flat = eid.reshape(M)
    w_flat = weight.reshape(M)
    tok_flat = jnp.repeat(jnp.arange(T, dtype=jnp.int32), 2)
    order = jnp.argsort(e_flat)
    e_s, tok_s, w_s = e_flat[order], tok_flat[order], w_flat[order]
    counts = jnp.zeros((N_EXPERTS,), jnp.int32).at[e_flat].add(1)
    starts = jnp.cumsum(counts) - counts
    padded = (counts + MOE_BLOCK - 1) // MOE_BLOCK * MOE_BLOCK
    pad_end = jnp.cumsum(padded)
    pad_start = pad_end - padded
    dest = pad_start[e_s] + (jnp.arange(M, dtype=jnp.int32) - starts[e_s])
    n_rows = (M + N_EXPERTS * (MOE_BLOCK - 1) + MOE_BLOCK - 1) // MOE_BLOCK * MOE_BLOCK
    n_blocks = n_rows // MOE_BLOCK
    row_tok = jnp.full((n_rows,), T, jnp.int32).at[dest].set(tok_s)
    row_w = jnp.zeros((n_rows,), f32).at[dest].set(w_s)
    blk_start = jnp.arange(n_blocks, dtype=jnp.int32) * MOE_BLOCK
    blk_exp = jnp.minimum(jnp.searchsorted(pad_end, blk_start, side='right'), N_EXPERTS - 1)
    x_pad = jnp.concatenate([xf, jnp.zeros((1, D), xf.dtype)], axis=0)

    def expert_block(args):
        e, toks, wts = args
        xb = x_pad[toks]
        hb = jax.nn.silu(xb @ w_gate[e]) * (xb @ w_up[e])
        return (hb @ w_down[e]) * wts[:, None].astype(xb.dtype)

    yb = lax.map(expert_block, (blk_exp, row_tok.reshape(n_blocks, MOE_BLOCK),
                                row_w.reshape(n_blocks, MOE_BLOCK)))
    y = jnp.zeros((T + 1, D), yb.dtype).at[row_tok].add(yb.reshape(n_rows, D))[:T]
    return y.reshape(B, S, D)


def setup_inputs(seed: int = 0) -> dict:
    key = jax.random.key(seed)
    ks = jax.random.split(key, 32)
    L = DEPTH
    f32 = jnp.float32

    def w(k, shape, fan_in):
        return jax.random.normal(k, shape, f32) * (fan_in ** -0.5)

    def gain(k, shape):
        return 1.0 + 0.05 * jax.random.normal(k, shape, f32)

    n_idx = jnp.arange(S5_STATE, dtype=f32)
    a_re = -0.5 + 0.01 * jax.random.normal(ks[6], (L, S5_GROUPS, S5_STATE), f32)
    a_im = math.pi * n_idx + 0.01 * jax.random.normal(ks[7], (L, S5_GROUPS, S5_STATE), f32)
    log_dt = jax.random.uniform(ks[8], (L, S5_GROUPS), f32, math.log(0.001), math.log(0.1))
    return {
        "x": jax.random.normal(ks[0], (BATCH, SEQ, D_MODEL), f32),
        "g_mix": gain(ks[1], (L, D_MODEL)),
        "w_in": w(ks[2], (L, D_MODEL, D_IN), D_MODEL),
        "g_kv": gain(ks[3], (L, KV_RANK)),
        "w_uk": w(ks[4], (L, KV_RANK, N_HEADS_A, HEAD_DIM_A), KV_RANK),
        "w_uv": w(ks[5], (L, KV_RANK, N_HEADS_A, HEAD_DIM_A), KV_RANK),
        "rel_bias": 0.2 * jax.random.normal(ks[9], (N_BUCKETS, N_HEADS_A), f32),
        "s5_a_re": a_re,
        "s5_a_im": a_im,
        "s5_log_dt": log_dt,
        "s5_b_re": w(ks[10], (L, S5_GROUPS, S5_STATE, S5_GROUP), 2 * S5_GROUP),
        "s5_b_im": w(ks[11], (L, S5_GROUPS, S5_STATE, S5_GROUP), 2 * S5_GROUP),
        "s5_c_re": w(ks[12], (L, S5_GROUPS, S5_GROUP, S5_STATE), S5_STATE),
        "s5_c_im": w(ks[13], (L, S5_GROUPS, S5_GROUP, S5_STATE), S5_STATE),
        "s5_d": jax.random.normal(ks[14], (L, S5_GROUPS, S5_GROUP), f32),
        "w_glu": w(ks[15], (L, S5_WIDTH, 2 * S5_WIDTH), S5_WIDTH),
        "w_up_a": w(ks[16], (L, ATTN_WIDTH, D_MODEL), ATTN_WIDTH),
        "w_up_b": w(ks[17], (L, S5_WIDTH, D_MODEL), S5_WIDTH),
        "b_gate": 0.01 * jax.random.normal(ks[18], (L, 2 * D_MODEL), f32),
        "w_out": w(ks[19], (L, D_MODEL, D_MODEL), D_MODEL),
        "g_ffn": gain(ks[20], (L, D_MODEL)),
        "w_router_g": w(ks[21], (L, D_MODEL, N_EXPERT_GROUPS), D_MODEL),
        "b_router_g": 0.01 * jax.random.normal(ks[22], (L, N_EXPERT_GROUPS), f32),
        "w_router_e": w(ks[23], (L, D_MODEL, N_EXPERTS), D_MODEL),
        "b_router_e": 0.01 * jax.random.normal(ks[24], (L, N_EXPERTS), f32),
        "w_gate": w(ks[25], (L, N_EXPERTS, D_MODEL, D_EXPERT), D_MODEL),
        "w_up": w(ks[26], (L, N_EXPERTS, D_MODEL, D_EXPERT), D_MODEL),
        "w_down": w(ks[27], (L, N_EXPERTS, D_EXPERT, D_MODEL), D_EXPERT),
        "g_final": gain(ks[28], (D_MODEL,)),
    }


def reference(x, g_mix, w_in, g_kv, w_uk, w_uv, rel_bias, s5_a_re, s5_a_im, s5_log_dt,
              s5_b_re, s5_b_im, s5_c_re, s5_c_im, s5_d, w_glu, w_up_a, w_up_b, b_gate, w_out,
              g_ffn, w_router_g, b_router_g, w_router_e, b_router_e, w_gate, w_up, w_down, g_final):
    B, S = x.shape[0], x.shape[1]
    for l in range(DEPTH):
        h = rmsnorm(x, g_mix[l])
        proj = h @ w_in[l]
        q, c_raw, qi, ki, wi, u, gl = jnp.split(proj, SPLIT_POINTS, axis=-1)
        q = q.reshape(B, S, N_HEADS_A, HEAD_DIM_A)
        c = rmsnorm(c_raw, g_kv[l])
        qi = qi.reshape(B, S, IDX_HEADS, IDX_DIM)
        wi = wi * IDX_SCALE
        y_a = dsa_attention(q, c, qi, ki, wi, w_uk[l], w_uv[l], rel_bias) @ w_up_a[l]
        y_b = s5_mixer(u, s5_a_re[l], s5_a_im[l], s5_log_dt[l], s5_b_re[l], s5_b_im[l],
                       s5_c_re[l], s5_c_im[l], s5_d[l], w_glu[l]) @ w_up_b[l]
        g_a, g_b = jnp.split(jax.nn.sigmoid(gl + b_gate[l]), 2, axis=-1)
        x = x + (g_a * y_a + g_b * y_b) @ w_out[l]
        h2 = rmsnorm(x, g_ffn[l])
        x = x + hier_moe(h2, w_router_g[l], b_router_g[l], w_router_e[l], b_router_e[l],
                         w_gate[l], w_up[l], w_down[l])
    return rmsnorm(x, g_final)
```

```python
import functools
import math

import jax
import jax.numpy as jnp
from jax import lax
from jax.experimental import pallas as pl
from jax.experimental.pallas import tpu as pltpu

F32 = jnp.float32
BF16 = jnp.bfloat16
I32 = jnp.int32

CHUNK = 64
TOPK_MAX = 256
IDX_SCALE = 0.0625
N_BUCKETS = 32
MAX_DISTANCE = 128
EPS = 1e-6
SPLIT_POINTS = (512, 640, 896, 960, 964, 1476)

LANES = 128
SUBLANES = 8
VMEM_LIMIT_BYTES = 56 * 1024 * 1024

ROW_TILE = 512
ATTN_TQ = 128
ATTN_TK = 256
S5_TIME_CHUNK = 64
S5_COL_BLOCK = 512
MOE_ROWS = 256

INT_MIN = -(2 ** 31)
NEG_MASK = -1e30
M_FLOOR = -1e20


def _dot(a, b):
    return jnp.dot(a, b, preferred_element_type=F32)


def _dot_nt(a, b):
    return lax.dot_general(a, b, (((1,), (1,)), ((), ())), preferred_element_type=F32)


def _rms(x, g):
    return x * lax.rsqrt(jnp.mean(x * x, axis=-1, keepdims=True) + EPS) * g


def _params(sem):
    return pltpu.CompilerParams(dimension_semantics=sem, vmem_limit_bytes=VMEM_LIMIT_BYTES)


def _const_spec(shape):
    n = len(shape)
    return pl.BlockSpec(shape, lambda *_: (0,) * n)


def _inproj_kernel(x_ref, g_ref, wq_ref, wuk_ref, wc_ref, gkv_ref, wqi_ref, wki_ref, wwi_ref, wu_ref,
                   wgl_ref, bg_ref, qlat_ref, c_ref, qi_ref, ki_ref, wit_ref, u_ref, gate_ref, *, q_scale):
    hb = _rms(x_ref[...], g_ref[...]).astype(BF16)
    qb = _dot(hb, wq_ref[...]).astype(BF16)
    n_pair = wuk_ref.shape[0]
    for j in range(n_pair):
        ql = _dot(qb[:, LANES * j:LANES * (j + 1)], wuk_ref[j]) * q_scale
        qlat_ref[2 * j] = ql[:, :LANES].astype(BF16)
        qlat_ref[2 * j + 1] = ql[:, LANES:].astype(BF16)
    c_ref[...] = _rms(_dot(hb, wc_ref[...]), gkv_ref[...]).astype(BF16)
    for h in range(wqi_ref.shape[0]):
        qi_ref[h] = _dot(hb, wqi_ref[h]).astype(BF16)
    ki_ref[...] = _dot(hb, wki_ref[...]).astype(BF16)
    wit_ref[...] = _dot_nt(wwi_ref[...], hb) * IDX_SCALE
    u_ref[...] = _dot(hb, wu_ref[...])
    gl = _dot(hb, wgl_ref[...]) + bg_ref[...]
    gate_ref[...] = 1.0 / (1.0 + jnp.exp(-gl))


def _inproj(x2, g_mix, w_in, g_kv, w_uk, b_gate):
    T, D = x2.shape
    R, H, dh = w_uk.shape
    sp = SPLIT_POINTS
    n_ih = sp[4] - sp[3]
    idx_dim = (sp[2] - sp[1]) // n_ih
    tm = ROW_TILE
    wb = w_in.astype(BF16)
    wq = wb[:, :sp[0]]
    wc = wb[:, sp[0]:sp[1]]
    wqi = wb[:, sp[1]:sp[2]].reshape(D, n_ih, idx_dim).transpose(1, 0, 2)
    wki = wb[:, sp[2]:sp[3]]
    wwi = jnp.zeros((SUBLANES, D), BF16).at[:n_ih].set(wb[:, sp[3]:sp[4]].T)
    wu = wb[:, sp[4]:sp[5]]
    wgl = wb[:, sp[5]:]
    wuk_t = jnp.transpose(w_uk, (1, 2, 0)).astype(BF16)
    z = jnp.zeros((dh, R), BF16)
    wuk_pairs = jnp.stack([
        jnp.concatenate([jnp.concatenate([wuk_t[2 * j], z], axis=1),
                         jnp.concatenate([z, wuk_t[2 * j + 1]], axis=1)], axis=0)
        for j in range(H // 2)])
    n_gate = wgl.shape[1]
    n_u = wu.shape[1]
    out_shape = (
        jax.ShapeDtypeStruct((H, T, R), BF16),
        jax.ShapeDtypeStruct((T, R), BF16),
        jax.ShapeDtypeStruct((n_ih, T, idx_dim), BF16),
        jax.ShapeDtypeStruct((T, idx_dim), BF16),
        jax.ShapeDtypeStruct((SUBLANES, T), F32),
        jax.ShapeDtypeStruct((T, n_u), F32),
        jax.ShapeDtypeStruct((T, n_gate), F32),
    )
    in_specs = [
        pl.BlockSpec((tm, D), lambda i: (i, 0)),
        _const_spec((1, D)), _const_spec(wq.shape), _const_spec(wuk_pairs.shape), _const_spec(wc.shape),
        _const_spec((1, R)), _const_spec(wqi.shape), _const_spec(wki.shape), _const_spec(wwi.shape),
        _const_spec(wu.shape), _const_spec(wgl.shape), _const_spec((1, n_gate)),
    ]
    out_specs = (
        pl.BlockSpec((H, tm, R), lambda i: (0, i, 0)),
        pl.BlockSpec((tm, R), lambda i: (i, 0)),
        pl.BlockSpec((n_ih, tm, idx_dim), lambda i: (0, i, 0)),
        pl.BlockSpec((tm, idx_dim), lambda i: (i, 0)),
        pl.BlockSpec((SUBLANES, tm), lambda i: (0, i)),
        pl.BlockSpec((tm, n_u), lambda i: (i, 0)),
        pl.BlockSpec((tm, n_gate), lambda i: (i, 0)),
    )
    return pl.pallas_call(
        functools.partial(_inproj_kernel, q_scale=dh ** -0.5),
        grid=(T // tm,), in_specs=in_specs, out_specs=out_specs, out_shape=out_shape,
        compiler_params=_params(("parallel",)), name="inproj",
    )(x2, g_mix.reshape(1, D), wq, wuk_pairs, wc, g_kv.reshape(1, R), wqi, wki, wwi, wu, wgl,
      b_gate.reshape(1, n_gate))


def _attn_kernel(qlat_ref, qi_ref, wit_ref, ki_ref, c_ref, ct_ref, bias_ref, tri_ref, wuv_ref, o_ref,
                 keys_ref, m_ref, l_ref, acc_ref, tie_ref, *, top_k):
    H, TQ, R = qlat_ref.shape
    n_ih = qi_ref.shape[0]
    TK = tri_ref.shape[0]
    j = pl.program_id(1)
    q0 = j * TQ
    n_kt = (q0 + TQ + TK - 1) // TK
    n_far = jnp.maximum(q0 - (MAX_DISTANCE - 1), 0) // TK

    lane = lax.broadcasted_iota(I32, (1, TQ), 1)
    limit = ((q0 + lane) // CHUNK + 1) * CHUNK

    qi_all = qi_ref[...].reshape(n_ih * TQ, qi_ref.shape[2])
    wt = wit_ref[...]

    def score_tile(kt, masked):
        k0 = pl.multiple_of(kt * TK, TK)
        sc = _dot_nt(ki_ref[pl.ds(k0, TK), :], qi_all)
        s = jnp.zeros((TK, TQ), F32)
        for h in range(n_ih):
            s = s + wt[h:h + 1, :] * jnp.maximum(sc[:, h * TQ:(h + 1) * TQ], 0.0)
        s = jnp.where(s == 0.0, 0.0, s)
        bits = lax.bitcast_convert_type(s, I32)
        key = bits ^ ((bits >> 31) & 0x7FFFFFFF)
        if masked:
            row = k0 + lax.broadcasted_iota(I32, (TK, 1), 0)
            key = jnp.where(row < limit, key, INT_MIN)
        keys_ref[pl.ds(k0, TK), :] = key

    def score_body(kt, carry):
        score_tile(kt, False)
        return carry

    lax.fori_loop(0, n_kt - 1, score_body, 0)
    score_tile(n_kt - 1, True)

    def count(pred):
        def body(kt, acc):
            blk = keys_ref[pl.ds(pl.multiple_of(kt * TK, TK), TK), :]
            ind = jnp.where(pred(blk), 1, 0).astype(I32)
            return acc + ind.reshape(TK // SUBLANES, SUBLANES, TQ).sum(axis=0)
        acc = lax.fori_loop(0, n_kt, body, jnp.zeros((SUBLANES, TQ), I32))
        return acc.sum(axis=0, keepdims=True)

    def bit_body(i, lo):
        cand = lo + lax.shift_left(jnp.int32(1), 31 - i)
        cnt = count(lambda blk: blk >= cand)
        return jnp.where(cnt >= top_k, cand, lo)

    thr = lax.fori_loop(0, 32, bit_body, jnp.full((1, TQ), INT_MIN, I32))
    n_gt = count(lambda blk: blk > thr)
    n_tie = jnp.where(thr == INT_MIN, 0, top_k - n_gt).astype(F32)

    qlat_all = qlat_ref[...].reshape(H * TQ, R)
    m_ref[...] = jnp.full(m_ref.shape, M_FLOOR, F32)
    l_ref[...] = jnp.zeros(l_ref.shape, F32)
    acc_ref[...] = jnp.zeros(acc_ref.shape, F32)
    tie_ref[...] = jnp.zeros(tie_ref.shape, F32)

    def flash_tile(kt, near):
        k0 = pl.multiple_of(kt * TK, TK)
        key = keys_ref[pl.ds(k0, TK), :]
        eq = key == thr
        rank = _dot(tri_ref[...], jnp.where(eq, 1.0, 0.0).astype(BF16)) + tie_ref[...]
        tie_ref[...] = rank[TK - 1:TK, :]
        negmask = jnp.where(key > thr, 0.0, jnp.where(eq, jnp.where(rank <= n_tie, 0.0, NEG_MASK), NEG_MASK))
        logits = _dot_nt(c_ref[pl.ds(k0, TK), :], qlat_all)
        if near:
            boff = pl.multiple_of(k0 - q0 + bias_ref.shape[1] // 2, LANES)
        ps = []
        alphas = []
        for h in range(H):
            sl = slice(h * TQ, (h + 1) * TQ)
            lg = logits[:, sl] + negmask
            if near:
                lg = lg + bias_ref[h, pl.ds(boff, TK), :]
            m_old = m_ref[:, sl]
            m_new = jnp.maximum(m_old, jnp.max(lg, axis=0, keepdims=True))
            alpha = jnp.exp(m_old - m_new)
            p = jnp.exp(lg - m_new)
            l_ref[:, sl] = alpha * l_ref[:, sl] + jnp.sum(p, axis=0, keepdims=True)
            m_ref[:, sl] = m_new
            ps.append(p.astype(BF16))
            alphas.append(alpha)
        p_all = jnp.concatenate(ps, axis=1)
        alpha_all = jnp.concatenate(alphas, axis=1)
        acc_ref[...] = acc_ref[...] * alpha_all + _dot(ct_ref[kt], p_all)

    def far_body(kt, carry):
        flash_tile(kt, False)
        return carry

    def near_body(kt, carry):
        flash_tile(kt, True)
        return carry

    lax.fori_loop(0, n_far, far_body, 0)
    lax.fori_loop(n_far, n_kt, near_body, 0)

    olat = acc_ref[...] * (1.0 / l_ref[...])
    o = jnp.zeros((TQ, wuv_ref.shape[2]), F32)
    for h in range(H):
        o = o + _dot(olat[:, h * TQ:(h + 1) * TQ].T.astype(BF16), wuv_ref[h])
    o_ref[...] = o.astype(BF16)


def _t5_bucket(rel):
    half = N_BUCKETS // 2
    max_exact = half // 2
    ret = jnp.where(rel > 0, half, 0)
    n = jnp.abs(rel)
    nf = jnp.maximum(n, 1).astype(F32)
    large = max_exact + (jnp.log(nf / max_exact) / math.log(MAX_DISTANCE / max_exact)
                         * (half - max_exact)).astype(I32)
    large = jnp.minimum(large, half - 1)
    return ret + jnp.where(n < max_exact, n, large)


def _attention(qlat, c, qi, ki, wit, w_uv, rel_bias, B, S):
    H, T, R = qlat.shape
    n_ih, _, idx_dim = qi.shape
    dh = w_uv.shape[2]
    TQ, TK = ATTN_TQ, ATTN_TK
    assert S % TK == 0 and S % TQ == 0 and TK % TQ == 0 and MAX_DISTANCE <= TQ
    top_k = min(TOPK_MAX, S // 4)
    n_qt = S // TQ
    half_rows = TK
    i = jnp.arange(2 * half_rows, dtype=I32)[:, None] - half_rows
    jq = jnp.arange(TQ, dtype=I32)[None, :]
    bt = rel_bias[_t5_bucket(i - jq)] - rel_bias[_t5_bucket(jnp.int32(-MAX_DISTANCE))][None, None, :]
    bias_t = jnp.transpose(bt, (2, 0, 1)).astype(F32)
    tri = (jnp.arange(TK)[:, None] >= jnp.arange(TK)[None, :]).astype(BF16)
    wuv_h = jnp.transpose(w_uv, (1, 0, 2)).astype(BF16)
    wuv_pad = jnp.zeros((H, R, H * dh), BF16)
    for h in range(H):
        wuv_pad = wuv_pad.at[h, :, h * dh:(h + 1) * dh].set(wuv_h[h])
    c3 = c.reshape(B, S, R)
    ct = jnp.transpose(c3.reshape(B, S // TK, TK, R), (0, 1, 3, 2))
    grid = (B, n_qt)
    in_specs = [
        pl.BlockSpec((H, None, TQ, R), lambda b, j: (0, b, j, 0)),
        pl.BlockSpec((n_ih, None, TQ, idx_dim), lambda b, j: (0, b, j, 0)),
        pl.BlockSpec((SUBLANES, TQ), lambda b, j: (0, b * n_qt + j)),
        pl.BlockSpec((None, S, idx_dim), lambda b, j: (b, 0, 0)),
        pl.BlockSpec((None, S, R), lambda b, j: (b, 0, 0)),
        pl.BlockSpec((None, S // TK, R, TK), lambda b, j: (b, 0, 0, 0)),
        _const_spec(bias_t.shape), _const_spec(tri.shape), _const_spec(wuv_pad.shape),
    ]
    return pl.pallas_call(
        functools.partial(_attn_kernel, top_k=top_k),
        grid=grid, in_specs=in_specs,
        out_specs=pl.BlockSpec((None, TQ, H * dh), lambda b, j: (b, j, 0)),
        out_shape=jax.ShapeDtypeStruct((B, S, H * dh), BF16),
        scratch_shapes=[pltpu.VMEM((S, TQ), I32), pltpu.VMEM((1, H * TQ), F32), pltpu.VMEM((1, H * TQ), F32),
                        pltpu.VMEM((R, H * TQ), F32), pltpu.VMEM((1, TQ), F32)],
        compiler_params=_params(("parallel", "arbitrary")), name="dsa_attention",
    )(qlat.reshape(H, B, S, R), qi.reshape(n_ih, B, S, idx_dim), wit, ki.reshape(B, S, idx_dim), c3, ct,
      bias_t, tri, wuv_pad)


def _s5_kernel(u_ref, ar_ref, ai_ref, bbr_ref, bbi_ref, ccr_ref, cci_ref, d_ref, wglu_ref, o_ref,
               xr_ref, xi_ref, sr_ref, si_ref, *, batch):
    rows, width = u_ref.shape
    n_state = ar_ref.shape[1]
    n_half = bbr_ref.shape[0]
    uw = width // n_half
    sw = n_state // n_half
    steps = rows // batch

    @pl.when(pl.program_id(0) == 0)
    def _():
        sr_ref[...] = jnp.zeros(sr_ref.shape, F32)
        si_ref[...] = jnp.zeros(si_ref.shape, F32)

    u = u_ref[...]
    ub = u.astype(BF16)
    for j in range(n_half):
        uj = ub[:, uw * j:uw * (j + 1)]
        xr_ref[:, sw * j:sw * (j + 1)] = _dot(uj, bbr_ref[j])
        xi_ref[:, sw * j:sw * (j + 1)] = _dot(uj, bbi_ref[j])

    cb = S5_COL_BLOCK
    for k in range(n_state // cb):
        cs = slice(cb * k, cb * (k + 1))
        ar = jnp.broadcast_to(ar_ref[:, cs], (batch, cb))
        ai = jnp.broadcast_to(ai_ref[:, cs], (batch, cb))

        def step(t, carry):
            xr, xi = carry
            r0 = pl.multiple_of(t * batch, batch)
            nr = ar * xr - ai * xi + xr_ref[pl.ds(r0, batch), cs]
            ni = ar * xi + ai * xr + xi_ref[pl.ds(r0, batch), cs]
            xr_ref[pl.ds(r0, batch), cs] = nr
            xi_ref[pl.ds(r0, batch), cs] = ni
            return nr, ni

        xr, xi = lax.fori_loop(0, steps, step, (sr_ref[:, cs], si_ref[:, cs]), unroll=8)
        sr_ref[:, cs] = xr
        si_ref[:, cs] = xi

    ys = []
    for j in range(n_half):
        xs_r = xr_ref[:, sw * j:sw * (j + 1)].astype(BF16)
        xs_i = xi_ref[:, sw * j:sw * (j + 1)].astype(BF16)
        ys.append(_dot(xs_r, ccr_ref[j]) - _dot(xs_i, cci_ref[j]))
    y = jnp.concatenate(ys, axis=1) + d_ref[...] * u
    z = 0.5 * y * (1.0 + jnp.tanh(math.sqrt(2.0 / math.pi) * (y + 0.044715 * (y * y * y))))
    vg = _dot(z.astype(BF16), wglu_ref[...])
    half = vg.shape[1] // 2
    o_ref[...] = (vg[:, :half] * (1.0 / (1.0 + jnp.exp(-vg[:, half:])))).astype(BF16)


def _block_diag_halves(blocks, n_half):
    G, r, c = blocks.shape
    g2 = G // n_half
    eye = jnp.eye(g2, dtype=blocks.dtype)
    b = blocks.reshape(n_half, g2, r, c)
    return jnp.einsum('jgrc,gh->jgrhc', b, eye).reshape(n_half, g2 * r, g2 * c)


def _s5(u_tb, a_re, a_im, log_dt, b_re, b_im, c_re, c_im, d_skip, w_glu, batch):
    rows_total, width = u_tb.shape
    G, N = a_re.shape
    P = b_re.shape[2]
    n_state = G * N
    n_half = 2
    dt = jnp.exp(log_dt.astype(F32))[:, None]
    ar, ai = a_re.astype(F32), a_im.astype(F32)
    mag = jnp.exp(ar * dt)
    abar_r = mag * jnp.cos(ai * dt)
    abar_i = mag * jnp.sin(ai * dt)
    nr, ni = abar_r - 1.0, abar_i
    den = ar * ar + ai * ai
    fr = (nr * ar + ni * ai) / den
    fi = (ni * ar - nr * ai) / den
    br_, bi_ = b_re.astype(F32), b_im.astype(F32)
    bb_r = fr[..., None] * br_ - fi[..., None] * bi_
    bb_i = fr[..., None] * bi_ + fi[..., None] * br_
    bbr = _block_diag_halves(jnp.transpose(bb_r, (0, 2, 1)), n_half).astype(BF16)
    bbi = _block_diag_halves(jnp.transpose(bb_i, (0, 2, 1)), n_half).astype(BF16)
    ccr = _block_diag_halves(jnp.transpose(c_re.astype(F32), (0, 2, 1)), n_half).astype(BF16)
    cci = _block_diag_halves(jnp.transpose(c_im.astype(F32), (0, 2, 1)), n_half).astype(BF16)
    rows = S5_TIME_CHUNK * batch
    assert batch % SUBLANES == 0 and rows_total % rows == 0 and n_state % S5_COL_BLOCK == 0
    in_specs = [
        pl.BlockSpec((rows, width), lambda i: (i, 0)),
        _const_spec((1, n_state)), _const_spec((1, n_state)),
        _const_spec(bbr.shape), _const_spec(bbi.shape), _const_spec(ccr.shape), _const_spec(cci.shape),
        _const_spec((1, width)), _const_spec(w_glu.shape),
    ]
    return pl.pallas_call(
        functools.partial(_s5_kernel, batch=batch),
        grid=(rows_total // rows,), in_specs=in_specs,
        out_specs=pl.BlockSpec((rows, width), lambda i: (i, 0)),
        out_shape=jax.ShapeDtypeStruct((rows_total, width), BF16),
        scratch_shapes=[pltpu.VMEM((rows, n_state), F32), pltpu.VMEM((rows, n_state), F32),
                        pltpu.VMEM((batch, n_state), F32), pltpu.VMEM((batch, n_state), F32)],
        compiler_params=_params(("arbitrary",)), name="s5_mixer",
    )(u_tb, abar_r.reshape(1, n_state), abar_i.reshape(1, n_state), bbr, bbi, ccr, cci,
      d_skip.astype(F32).reshape(1, width), w_glu.astype(BF16))


def _mix_kernel(o_ref, s_ref, gate_ref, x_ref, wa_ref, wb_ref, wo_ref, gf_ref, wr_ref, br_ref,
                x1_ref, h2_ref, route_ref, *, n_groups, per_group):
    d = x_ref.shape[1]
    tm = x_ref.shape[0]
    ya = _dot(o_ref[...], wa_ref[...])
    yb = _dot(s_ref[...], wb_ref[...])
    mix = gate_ref[:, :d] * ya + gate_ref[:, d:] * yb
    x1 = x_ref[...] + _dot(mix.astype(BF16), wo_ref[...])
    x1_ref[...] = x1
    h2b = _rms(x1, gf_ref[...]).astype(BF16)
    h2_ref[...] = h2b
    lt = _dot_nt(wr_ref[...], h2b) + br_ref[...]
    n_exp = n_groups * per_group
    lg = lt[n_exp:n_exp + n_groups, :]
    eg = jnp.exp(lg - jnp.max(lg, axis=0, keepdims=True))
    pg = eg / jnp.sum(eg, axis=0, keepdims=True)
    gval = jnp.max(pg, axis=0, keepdims=True)
    gi = lax.broadcasted_iota(I32, (n_groups, tm), 0)
    grp = jnp.min(jnp.where(pg == gval, gi, n_groups), axis=0, keepdims=True)
    le = jnp.zeros((per_group, tm), F32)
    for g in range(n_groups):
        le = jnp.where(grp == g, lt[per_group * g:per_group * (g + 1), :], le)
    ee = jnp.exp(le - jnp.max(le, axis=0, keepdims=True))
    pe = ee / jnp.sum(ee, axis=0, keepdims=True)
    ei = lax.broadcasted_iota(I32, (per_group, tm), 0)
    p0 = jnp.max(pe, axis=0, keepdims=True)
    i0 = jnp.min(jnp.where(pe == p0, ei, per_group), axis=0, keepdims=True)
    pe2 = jnp.where(ei == i0, -1.0, pe)
    p1 = jnp.max(pe2, axis=0, keepdims=True)
    i1 = jnp.min(jnp.where(pe2 == p1, ei, per_group), axis=0, keepdims=True)
    psum = p0 + p1
    w0 = gval * (p0 / psum)
    w1 = gval * (p1 / psum)
    e0 = (grp * per_group + i0).astype(F32)
    e1 = (grp * per_group + i1).astype(F32)
    zero = jnp.zeros((SUBLANES - 4, tm), F32)
    route_ref[...] = jnp.concatenate([e0, e1, w0, w1, zero], axis=0)


def _mix(o, s5g, gates, x2, w_up_a, w_up_b, w_out, g_ffn, w_rg, b_rg, w_re, b_re):
    T, D = x2.shape
    tm = ROW_TILE
    n_groups = w_rg.shape[1]
    n_exp = w_re.shape[1]
    per_group = n_exp // n_groups
    n_r = -(-(n_exp + n_groups) // SUBLANES) * SUBLANES
    wr = jnp.zeros((n_r, D), BF16).at[:n_exp].set(w_re.T.astype(BF16)).at[n_exp:n_exp + n_groups].set(
        w_rg.T.astype(BF16))
    br = jnp.zeros((n_r, 1), F32).at[:n_exp, 0].set(b_re.astype(F32)).at[n_exp:n_exp + n_groups, 0].set(
        b_rg.astype(F32))
    wa, wb_, wo = w_up_a.astype(BF16), w_up_b.astype(BF16), w_out.astype(BF16)
    in_specs = [
        pl.BlockSpec((tm, o.shape[1]), lambda i: (i, 0)),
        pl.BlockSpec((tm, s5g.shape[1]), lambda i: (i, 0)),
        pl.BlockSpec((tm, gates.shape[1]), lambda i: (i, 0)),
        pl.BlockSpec((tm, D), lambda i: (i, 0)),
        _const_spec(wa.shape), _const_spec(wb_.shape), _const_spec(wo.shape), _const_spec((1, D)),
        _const_spec(wr.shape), _const_spec(br.shape),
    ]
    out_shape = (jax.ShapeDtypeStruct((T, D), F32), jax.ShapeDtypeStruct((T, D), BF16),
                 jax.ShapeDtypeStruct((SUBLANES, T), F32))
    out_specs = (pl.BlockSpec((tm, D), lambda i: (i, 0)), pl.BlockSpec((tm, D), lambda i: (i, 0)),
                 pl.BlockSpec((SUBLANES, tm), lambda i: (0, i)))
    return pl.pallas_call(
        functools.partial(_mix_kernel, n_groups=n_groups, per_group=per_group),
        grid=(T // tm,), in_specs=in_specs, out_specs=out_specs, out_shape=out_shape,
        compiler_params=_params(("parallel",)), name="mix_router",
    )(o, s5g, gates, x2, wa, wb_, wo, g_ffn.reshape(1, D), wr, br)


def _moe_kernel(be_ref, x_ref, rw_ref, wg_ref, wu_ref, wd_ref, y_ref):
    del be_ref
    xb = x_ref[...]
    g = _dot(xb, wg_ref[...])
    hb = (g * (1.0 / (1.0 + jnp.exp(-g)))) * _dot(xb, wu_ref[...])
    y_ref[...] = _dot(hb.astype(BF16), wd_ref[...]) * rw_ref[...]


def _moe_ffn(xs, row_w, blk_exp, w_gate, w_up, w_down):
    n_rows, D = xs.shape
    E, _, De = w_gate.shape
    bm = MOE_ROWS
    grid_spec = pltpu.PrefetchScalarGridSpec(
        num_scalar_prefetch=1, grid=(n_rows // bm,),
        in_specs=[
            pl.BlockSpec((bm, D), lambda i, be: (i, 0)),
            pl.BlockSpec((bm, 1), lambda i, be: (i, 0)),
            pl.BlockSpec((None, D, De), lambda i, be: (be[i], 0, 0)),
            pl.BlockSpec((None, D, De), lambda i, be: (be[i], 0, 0)),
            pl.BlockSpec((None, De, D), lambda i, be: (be[i], 0, 0)),
        ],
        out_specs=pl.BlockSpec((bm, D), lambda i, be: (i, 0)),
    )
    return pl.pallas_call(
        _moe_kernel, grid_spec=grid_spec, out_shape=jax.ShapeDtypeStruct((n_rows, D), F32),
        compiler_params=_params(("arbitrary",)), name="moe_ffn",
    )(blk_exp, xs, row_w, w_gate.astype(BF16), w_up.astype(BF16), w_down.astype(BF16))


def _final_kernel(x1_ref, y0_ref, y1_ref, g_ref, o_ref):
    o_ref[...] = _rms(x1_ref[...] + (y0_ref[...] + y1_ref[...]), g_ref[...])


def _final(x1, y0, y1, g_final):
    T, D = x1.shape
    tm = ROW_TILE
    spec = pl.BlockSpec((tm, D), lambda i: (i, 0))
    return pl.pallas_call(
        _final_kernel, grid=(T // tm,), in_specs=[spec, spec, spec, _const_spec((1, D))], out_specs=spec,
        out_shape=jax.ShapeDtypeStruct((T, D), F32), compiler_params=_params(("parallel",)), name="final_norm",
    )(x1, y0, y1, g_final.reshape(1, D))


def _dispatch_plan(eid, weight, n_experts, bm):
    T = eid.shape[0]
    M = 2 * T
    e_flat = eid.reshape(M)
    order = jnp.argsort(e_flat)
    e_s = e_flat[order]
    counts = jnp.zeros((n_experts,), I32).at[e_flat].add(1)
    starts = jnp.cumsum(counts) - counts
    padded = (counts + bm - 1) // bm * bm
    pad_end = jnp.cumsum(padded)
    pad_start = pad_end - padded
    dest = pad_start[e_s] + (jnp.arange(M, dtype=I32) - starts[e_s])
    n_rows = (M + n_experts * (bm - 1) + bm - 1) // bm * bm
    tok_flat = jnp.repeat(jnp.arange(T, dtype=I32), 2)
    row_tok = jnp.full((n_rows,), T, I32).at[dest].set(tok_flat[order])
    row_w = jnp.zeros((n_rows,), F32).at[dest].set(weight.reshape(M)[order])
    pos = jnp.zeros((M,), I32).at[order].set(dest)
    blk_start = jnp.arange(n_rows // bm, dtype=I32) * bm
    blk_exp = jnp.minimum(jnp.searchsorted(pad_end, blk_start, side='right'), n_experts - 1).astype(I32)
    return row_tok, row_w, pos.reshape(T, 2), blk_exp


def kernel(x, g_mix, w_in, g_kv, w_uk, w_uv, rel_bias, s5_a_re, s5_a_im, s5_log_dt, s5_b_re, s5_b_im,
           s5_c_re, s5_c_im, s5_d, w_glu, w_up_a, w_up_b, b_gate, w_out, g_ffn, w_router_g, b_router_g,
           w_router_e, b_router_e, w_gate, w_up, w_down, g_final):
    B, S, D = x.shape
    T = B * S
    assert T % ROW_TILE == 0 and g_mix.shape[0] == 1
    x2 = x.reshape(T, D)
    for l in range(g_mix.shape[0]):
        qlat, c, qi, ki, wit, u, gates = _inproj(x2, g_mix[l], w_in[l], g_kv[l], w_uk[l], b_gate[l])
        o = _attention(qlat, c, qi, ki, wit, w_uv[l], rel_bias, B, S).reshape(T, -1)
        u_tb = jnp.transpose(u.reshape(B, S, -1), (1, 0, 2)).reshape(T, -1)
        s5_tb = _s5(u_tb, s5_a_re[l], s5_a_im[l], s5_log_dt[l], s5_b_re[l], s5_b_im[l], s5_c_re[l],
                    s5_c_im[l], s5_d[l], w_glu[l], B)
        s5g = jnp.transpose(s5_tb.reshape(S, B, -1), (1, 0, 2)).reshape(T, -1)
        x1, h2, route = _mix(o, s5g, gates, x2, w_up_a[l], w_up_b[l], w_out[l], g_ffn[l],
                             w_router_g[l], b_router_g[l], w_router_e[l], b_router_e[l])
        eid = jnp.transpose(route[0:2]).astype(I32)
        weight = jnp.transpose(route[2:4])
        n_experts = w_gate.shape[1]
        row_tok, row_w, pos, blk_exp = _dispatch_plan(eid, weight, n_experts, MOE_ROWS)
        h2_pad = jnp.concatenate([h2, jnp.zeros((1, D), h2.dtype)], axis=0)
        yb = _moe_ffn(h2_pad[row_tok], row_w[:, None], blk_exp, w_gate[l], w_up[l], w_down[l])
        x2 = x1
        y0, y1 = yb[pos[:, 0]], yb[pos[:, 1]]
    out = _final(x2, y0, y1, g_final)
    return out.reshape(B, S, D)
```

```python
import functools
import math

import jax
import jax.numpy as jnp
from jax import lax
from jax.experimental import pallas as pl
from jax.experimental.pallas import tpu as pltpu

F32 = jnp.float32
BF16 = jnp.bfloat16
I32 = jnp.int32

CHUNK = 64
TOPK_MAX = 256
IDX_SCALE = 0.0625
N_BUCKETS = 32
MAX_DISTANCE = 128
EPS = 1e-6
SPLIT_POINTS = (512, 640, 896, 960, 964, 1476)

LANES = 128
SUBLANES = 8
VMEM_LIMIT_BYTES = 56 * 1024 * 1024

ROW_TILE = 512
ATTN_TQ = 128
ATTN_TK = 256
S5_TIME_CHUNK = 64
S5_COL_BLOCK = 512
MOE_ROWS = 256

INT_MIN = -(2 ** 31)
LOG2E = math.log2(math.e)
ONES_ROWS = 16
NEG_MASK = -1e30
M_FLOOR = -1e20


def _dot(a, b):
    return jnp.dot(a, b, preferred_element_type=F32)


def _dot_nt(a, b):
    return lax.dot_general(a, b, (((1,), (1,)), ((), ())), preferred_element_type=F32)


def _rms(x, g):
    return x * lax.rsqrt(jnp.mean(x * x, axis=-1, keepdims=True) + EPS) * g


def _params(sem):
    return pltpu.CompilerParams(dimension_semantics=sem, vmem_limit_bytes=VMEM_LIMIT_BYTES)


def _const_spec(shape):
    n = len(shape)
    return pl.BlockSpec(shape, lambda *_: (0,) * n)


def _inproj_kernel(x_ref, g_ref, wq_ref, wuk_ref, wc_ref, gkv_ref, wqi_ref, wki_ref, wwi_ref, wu_ref,
                   wgl_ref, bg_ref, qlat_ref, c_ref, qi_ref, ki_ref, wit_ref, u_ref, gate_ref, *, q_scale):
    hb = _rms(x_ref[...], g_ref[...]).astype(BF16)
    qb = _dot(hb, wq_ref[...]).astype(BF16)
    n_pair = wuk_ref.shape[0]
    for j in range(n_pair):
        ql = _dot(qb[:, LANES * j:LANES * (j + 1)], wuk_ref[j]) * q_scale
        qlat_ref[2 * j] = ql[:, :LANES].astype(BF16)
        qlat_ref[2 * j + 1] = ql[:, LANES:].astype(BF16)
    c_ref[...] = _rms(_dot(hb, wc_ref[...]), gkv_ref[...]).astype(BF16)
    for h in range(wqi_ref.shape[0]):
        qi_ref[h] = _dot(hb, wqi_ref[h]).astype(BF16)
    ki_ref[...] = _dot(hb, wki_ref[...]).astype(BF16)
    wit_ref[...] = _dot_nt(wwi_ref[...], hb) * IDX_SCALE
    u_ref[...] = _dot(hb, wu_ref[...])
    gl = _dot(hb, wgl_ref[...]) + bg_ref[...]
    gate_ref[...] = 1.0 / (1.0 + jnp.exp(-gl))


def _inproj(x2, g_mix, w_in, g_kv, w_uk, b_gate, B, S):
    T, D = x2.shape
    R, H, dh = w_uk.shape
    sp = SPLIT_POINTS
    n_ih = sp[4] - sp[3]
    idx_dim = (sp[2] - sp[1]) // n_ih
    tm = ROW_TILE
    spt = S // tm
    wb = w_in.astype(BF16)
    wq = wb[:, :sp[0]]
    wc = wb[:, sp[0]:sp[1]]
    wqi = wb[:, sp[1]:sp[2]].reshape(D, n_ih, idx_dim).transpose(1, 0, 2)
    wki = wb[:, sp[2]:sp[3]]
    wwi = jnp.zeros((SUBLANES, D), BF16).at[:n_ih].set(wb[:, sp[3]:sp[4]].T)
    wu = wb[:, sp[4]:sp[5]]
    wgl = wb[:, sp[5]:]
    wuk_t = jnp.transpose(w_uk, (1, 2, 0)).astype(BF16)
    z = jnp.zeros((dh, R), BF16)
    wuk_pairs = jnp.stack([
        jnp.concatenate([jnp.concatenate([wuk_t[2 * j], z], axis=1),
                         jnp.concatenate([z, wuk_t[2 * j + 1]], axis=1)], axis=0)
        for j in range(H // 2)])
    n_gate = wgl.shape[1]
    n_u = wu.shape[1]
    out_shape = (
        jax.ShapeDtypeStruct((H, T, R), BF16),
        jax.ShapeDtypeStruct((T, R), BF16),
        jax.ShapeDtypeStruct((n_ih, T, idx_dim), BF16),
        jax.ShapeDtypeStruct((T, idx_dim), BF16),
        jax.ShapeDtypeStruct((SUBLANES, T), F32),
        jax.ShapeDtypeStruct((S, B * n_u), F32),
        jax.ShapeDtypeStruct((T, n_gate), F32),
    )
    in_specs = [
        pl.BlockSpec((tm, D), lambda i: (i, 0)),
        _const_spec((1, D)), _const_spec(wq.shape), _const_spec(wuk_pairs.shape), _const_spec(wc.shape),
        _const_spec((1, R)), _const_spec(wqi.shape), _const_spec(wki.shape), _const_spec(wwi.shape),
        _const_spec(wu.shape), _const_spec(wgl.shape), _const_spec((1, n_gate)),
    ]
    out_specs = (
        pl.BlockSpec((H, tm, R), lambda i: (0, i, 0)),
        pl.BlockSpec((tm, R), lambda i: (i, 0)),
        pl.BlockSpec((n_ih, tm, idx_dim), lambda i: (0, i, 0)),
        pl.BlockSpec((tm, idx_dim), lambda i: (i, 0)),
        pl.BlockSpec((SUBLANES, tm), lambda i: (0, i)),
        pl.BlockSpec((tm, n_u), lambda i: (i % spt, i // spt)),
        pl.BlockSpec((tm, n_gate), lambda i: (i, 0)),
    )
    return pl.pallas_call(
        functools.partial(_inproj_kernel, q_scale=dh ** -0.5 * LOG2E),
        grid=(T // tm,), in_specs=in_specs, out_specs=out_specs, out_shape=out_shape,
        compiler_params=_params(("parallel",)), name="inproj",
    )(x2, g_mix.reshape(1, D), wq, wuk_pairs, wc, g_kv.reshape(1, R), wqi, wki, wwi, wu, wgl,
      b_gate.reshape(1, n_gate))


def _attn_kernel(qlat_ref, qi_ref, wit_ref, ki_ref, c_ref, ct_ref, bias_ref, tri_ref, eye_ref, wuv_ref, o_ref,
                 keys_ref, mask_ref, qa_ref, lga_ref, lgb_ref, tma_ref, tmb_ref, m_ref, acc_ref, tie_ref, *, top_k):
    H, TQ, R = qlat_ref.shape
    n_ih = qi_ref.shape[0]
    TK = tri_ref.shape[0]
    j = pl.program_id(1)
    q0 = j * TQ
    n_kt = (q0 + TQ + TK - 1) // TK
    n_pair = (n_kt + 1) // 2

    lane = lax.broadcasted_iota(I32, (1, TQ), 1)
    limit = ((q0 + lane) // CHUNK + 1) * CHUNK

    qi_all = qi_ref[...].reshape(n_ih * TQ, qi_ref.shape[2])
    wt = wit_ref[...]

    def score_tile(kt):
        k0 = pl.multiple_of(kt * TK, TK)
        sc = _dot_nt(ki_ref[pl.ds(k0, TK), :], qi_all)
        s = jnp.zeros((TK, TQ), F32)
        for h in range(n_ih):
            s = s + wt[h:h + 1, :] * jnp.maximum(sc[:, h * TQ:(h + 1) * TQ], 0.0)
        s = jnp.where(s == 0.0, 0.0, s)
        bits = lax.bitcast_convert_type(s, I32)
        key = bits ^ ((bits >> 31) & 0x7FFFFFFF)
        row = k0 + lax.broadcasted_iota(I32, (TK, 1), 0)
        keys_ref[pl.ds(k0, TK), :] = jnp.where(row < limit, key, INT_MIN)

    def score_pair(i, carry):
        score_tile(2 * i)
        score_tile(2 * i + 1)
        return carry

    lax.fori_loop(0, n_pair, score_pair, 0)

    def count(pred):
        def body(kt, acc):
            blk = keys_ref[pl.ds(pl.multiple_of(kt * TK, TK), TK), :]
            ind = jnp.where(pred(blk), 1, 0).astype(I32)
            return acc + ind.reshape(TK // SUBLANES, SUBLANES, TQ).sum(axis=0)
        acc = lax.fori_loop(0, n_kt, body, jnp.zeros((SUBLANES, TQ), I32))
        return acc.sum(axis=0, keepdims=True)

    def bit_body(i, lo):
        cand = lo + lax.shift_left(jnp.int32(1), 31 - i)
        cnt = count(lambda blk: blk >= cand)
        return jnp.where(cnt >= top_k, cand, lo)

    thr = lax.fori_loop(0, 32, bit_body, jnp.full((1, TQ), INT_MIN, I32))
    n_gt = count(lambda blk: blk > thr)
    n_tie = jnp.where(thr == INT_MIN, 0, top_k - n_gt).astype(F32)

    tie_ref[...] = jnp.zeros(tie_ref.shape, F32)

    def mask_pair(i, carry):
        ks = [pl.multiple_of((2 * i + t) * TK, TK) for t in range(2)]
        keys = [keys_ref[pl.ds(k0, TK), :] for k0 in ks]
        eqs = [key == thr for key in keys]
        ranks = [_dot(tri_ref[...], jnp.where(eq, 1.0, 0.0).astype(BF16)) for eq in eqs]
        tie = tie_ref[...]
        for k0, key, eq, rank in zip(ks, keys, eqs, ranks):
            rank = rank + tie
            tie = rank[TK - 1:TK, :]
            negmask = jnp.where(key > thr, 0.0,
                                jnp.where(eq, jnp.where(rank <= n_tie, 0.0, NEG_MASK), NEG_MASK))
            mask_ref[pl.ds(k0, TK), :] = negmask.astype(BF16)
        tie_ref[...] = tie
        return carry

    lax.fori_loop(0, n_pair, mask_pair, 0)

    qa_ref[...] = jnp.concatenate([qlat_ref[...].reshape(H * TQ, R), eye_ref[...]], axis=1)
    m_ref[...] = jnp.full(m_ref.shape, M_FLOOR, F32)
    acc_ref[...] = jnp.zeros(acc_ref.shape, F32)

    def issue_logits(kt, lg_ref, tm_ref):
        k0 = pl.multiple_of(jnp.minimum(kt, 2 * n_pair - 1) * TK, TK)
        c_aug = jnp.concatenate([c_ref[pl.ds(k0, TK), :], mask_ref[pl.ds(k0, TK), :]], axis=1)
        logits = _dot_nt(c_aug, qa_ref[...])
        boff = pl.multiple_of(jnp.clip(k0 - q0 + 2 * TK, 0, 2 * TK), LANES)
        for h in range(H):
            sl = slice(h * TQ, (h + 1) * TQ)
            lg = logits[:, sl] + bias_ref[h, pl.ds(boff, TK), :]
            lg_ref[:, sl] = lg
            tm_ref[:, sl] = jnp.max(lg, axis=0, keepdims=True)

    def softmax_pv(kt, lg_ref, tm_ref):
        ct_tile = ct_ref[kt]
        for hp in range(H // 2):
            cols = slice(2 * hp * TQ, 2 * (hp + 1) * TQ)
            m_old = m_ref[:, cols]
            m_new = jnp.maximum(m_old, tm_ref[:, cols])
            m_ref[:, cols] = m_new
            alpha = jnp.exp2(m_old - m_new)
            p = jnp.exp2(lg_ref[:, cols] - m_new).astype(BF16)
            acc_ref[:, cols] = acc_ref[:, cols] * alpha + _dot(ct_tile, p)

    issue_logits(0, lga_ref, tma_ref)

    def flash_pair(i):
        issue_logits(2 * i + 1, lgb_ref, tmb_ref)
        softmax_pv(2 * i, lga_ref, tma_ref)
        issue_logits(2 * i + 2, lga_ref, tma_ref)
        softmax_pv(2 * i + 1, lgb_ref, tmb_ref)

    def flash_quad(i, carry):
        flash_pair(2 * i)
        flash_pair(2 * i + 1)
        return carry

    def flash_rest(i, carry):
        flash_pair(i)
        return carry

    n_quad = n_pair // 2
    lax.fori_loop(0, n_quad, flash_quad, 0)
    lax.fori_loop(2 * n_quad, n_pair, flash_rest, 0)

    olat = acc_ref[:R, :] * (1.0 / acc_ref[R:R + 1, :])
    o = jnp.zeros((TQ, wuv_ref.shape[2]), F32)
    for h in range(H):
        o = o + _dot(olat[:, h * TQ:(h + 1) * TQ].T.astype(BF16), wuv_ref[h])
    o_ref[...] = o.astype(BF16)


def _t5_bucket(rel):
    half = N_BUCKETS // 2
    max_exact = half // 2
    ret = jnp.where(rel > 0, half, 0)
    n = jnp.abs(rel)
    nf = jnp.maximum(n, 1).astype(F32)
    large = max_exact + (jnp.log(nf / max_exact) / math.log(MAX_DISTANCE / max_exact)
                         * (half - max_exact)).astype(I32)
    large = jnp.minimum(large, half - 1)
    return ret + jnp.where(n < max_exact, n, large)


def _attention(qlat, c, qi, ki, wit, w_uv, rel_bias, B, S):
    H, T, R = qlat.shape
    n_ih, _, idx_dim = qi.shape
    dh = w_uv.shape[2]
    TQ, TK = ATTN_TQ, ATTN_TK
    assert S % (2 * TK) == 0 and S % TQ == 0 and TK % TQ == 0 and MAX_DISTANCE <= TQ
    top_k = min(TOPK_MAX, S // 4)
    n_qt = S // TQ
    i = jnp.arange(3 * TK, dtype=I32)[:, None] - 2 * TK
    jq = jnp.arange(TQ, dtype=I32)[None, :]
    bt = rel_bias[_t5_bucket(i - jq)] - rel_bias[_t5_bucket(jnp.int32(-MAX_DISTANCE))][None, None, :]
    bias_t = (jnp.transpose(bt, (2, 0, 1)) * LOG2E).astype(F32)
    tri = (jnp.arange(TK)[:, None] >= jnp.arange(TK)[None, :]).astype(BF16)
    eye = jnp.tile(jnp.eye(TQ, dtype=BF16), (H, 1))
    wuv_h = jnp.transpose(w_uv, (1, 0, 2)).astype(BF16)
    wuv_pad = jnp.zeros((H, R, H * dh), BF16)
    for h in range(H):
        wuv_pad = wuv_pad.at[h, :, h * dh:(h + 1) * dh].set(wuv_h[h])
    c3 = c.reshape(B, S, R)
    ct = jnp.transpose(c3.reshape(B, S // TK, TK, R), (0, 1, 3, 2))
    ones_rows = jnp.zeros((B, S // TK, ONES_ROWS, TK), BF16).at[:, :, 0, :].set(1.0)
    ct = jnp.concatenate([ct, ones_rows], axis=2)
    grid = (B, n_qt)
    in_specs = [
        pl.BlockSpec((H, None, TQ, R), lambda b, j: (0, b, j, 0)),
        pl.BlockSpec((n_ih, None, TQ, idx_dim), lambda b, j: (0, b, j, 0)),
        pl.BlockSpec((SUBLANES, TQ), lambda b, j: (0, b * n_qt + j)),
        pl.BlockSpec((None, S, idx_dim), lambda b, j: (b, 0, 0)),
        pl.BlockSpec((None, S, R), lambda b, j: (b, 0, 0)),
        pl.BlockSpec((None, S // TK, R + ONES_ROWS, TK), lambda b, j: (b, 0, 0, 0)),
        _const_spec(bias_t.shape), _const_spec(tri.shape), _const_spec(eye.shape), _const_spec(wuv_pad.shape),
    ]
    return pl.pallas_call(
        functools.partial(_attn_kernel, top_k=top_k),
        grid=grid, in_specs=in_specs,
        out_specs=pl.BlockSpec((None, TQ, H * dh), lambda b, j: (b, j, 0)),
        out_shape=jax.ShapeDtypeStruct((B, S, H * dh), BF16),
        scratch_shapes=[pltpu.VMEM((S, TQ), I32), pltpu.VMEM((S, TQ), BF16),
                        pltpu.VMEM((H * TQ, R + TQ), BF16), pltpu.VMEM((TK, H * TQ), F32),
                        pltpu.VMEM((TK, H * TQ), F32), pltpu.VMEM((1, H * TQ), F32), pltpu.VMEM((1, H * TQ), F32),
                        pltpu.VMEM((1, H * TQ), F32),
                        pltpu.VMEM((R + ONES_ROWS, H * TQ), F32), pltpu.VMEM((1, TQ), F32)],
        compiler_params=_params(("parallel", "arbitrary")), name="dsa_attention",
    )(qlat.reshape(H, B, S, R), qi.reshape(n_ih, B, S, idx_dim), wit, ki.reshape(B, S, idx_dim), c3, ct,
      bias_t, tri, eye, wuv_pad)


def _s5_kernel(u_ref, ar_ref, ai_ref, bbr_ref, bbi_ref, ccr_ref, cci_ref, d_ref, wglu_ref, o_ref,
               xr_ref, xi_ref, sr_ref, si_ref, *, batch):
    rows, width = u_ref.shape
    n_state = ar_ref.shape[1]
    n_half = bbr_ref.shape[0]
    uw = width // n_half
    sw = n_state // n_half
    steps = rows // batch

    @pl.when(pl.program_id(0) == 0)
    def _():
        sr_ref[...] = jnp.zeros(sr_ref.shape, F32)
        si_ref[...] = jnp.zeros(si_ref.shape, F32)

    u = u_ref[...]
    ub = u.astype(BF16)
    for j in range(n_half):
        uj = ub[:, uw * j:uw * (j + 1)]
        xr_ref[:, sw * j:sw * (j + 1)] = _dot(uj, bbr_ref[j])
        xi_ref[:, sw * j:sw * (j + 1)] = _dot(uj, bbi_ref[j])

    cb = S5_COL_BLOCK
    for k in range(n_state // cb):
        cs = slice(cb * k, cb * (k + 1))
        ar = jnp.broadcast_to(ar_ref[:, cs], (batch, cb))
        ai = jnp.broadcast_to(ai_ref[:, cs], (batch, cb))

        def step(t, carry):
            xr, xi = carry
            r0 = pl.multiple_of(t * batch, batch)
            nr = ar * xr - ai * xi + xr_ref[pl.ds(r0, batch), cs]
            ni = ar * xi + ai * xr + xi_ref[pl.ds(r0, batch), cs]
            xr_ref[pl.ds(r0, batch), cs] = nr
            xi_ref[pl.ds(r0, batch), cs] = ni
            return nr, ni

        xr, xi = lax.fori_loop(0, steps, step, (sr_ref[:, cs], si_ref[:, cs]), unroll=8)
        sr_ref[:, cs] = xr
        si_ref[:, cs] = xi

    ys = []
    for j in range(n_half):
        xs_r = xr_ref[:, sw * j:sw * (j + 1)].astype(BF16)
        xs_i = xi_ref[:, sw * j:sw * (j + 1)].astype(BF16)
        ys.append(_dot(xs_r, ccr_ref[j]) - _dot(xs_i, cci_ref[j]))
    y = jnp.concatenate(ys, axis=1) + d_ref[...] * u
    z = 0.5 * y * (1.0 + jnp.tanh(math.sqrt(2.0 / math.pi) * (y + 0.044715 * (y * y * y))))
    vg = _dot(z.astype(BF16), wglu_ref[...])
    half = vg.shape[1] // 2
    o_ref[...] = (vg[:, :half] * (1.0 / (1.0 + jnp.exp(-vg[:, half:])))).astype(BF16)


def _block_diag_halves(blocks, n_half):
    G, r, c = blocks.shape
    g2 = G // n_half
    eye = jnp.eye(g2, dtype=blocks.dtype)
    b = blocks.reshape(n_half, g2, r, c)
    return jnp.einsum('jgrc,gh->jgrhc', b, eye).reshape(n_half, g2 * r, g2 * c)


def _s5(u_tb, a_re, a_im, log_dt, b_re, b_im, c_re, c_im, d_skip, w_glu, batch):
    rows_total, width = u_tb.shape
    G, N = a_re.shape
    P = b_re.shape[2]
    n_state = G * N
    n_half = 2
    dt = jnp.exp(log_dt.astype(F32))[:, None]
    ar, ai = a_re.astype(F32), a_im.astype(F32)
    mag = jnp.exp(ar * dt)
    abar_r = mag * jnp.cos(ai * dt)
    abar_i = mag * jnp.sin(ai * dt)
    nr, ni = abar_r - 1.0, abar_i
    den = ar * ar + ai * ai
    fr = (nr * ar + ni * ai) / den
    fi = (ni * ar - nr * ai) / den
    br_, bi_ = b_re.astype(F32), b_im.astype(F32)
    bb_r = fr[..., None] * br_ - fi[..., None] * bi_
    bb_i = fr[..., None] * bi_ + fi[..., None] * br_
    bbr = _block_diag_halves(jnp.transpose(bb_r, (0, 2, 1)), n_half).astype(BF16)
    bbi = _block_diag_halves(jnp.transpose(bb_i, (0, 2, 1)), n_half).astype(BF16)
    ccr = _block_diag_halves(jnp.transpose(c_re.astype(F32), (0, 2, 1)), n_half).astype(BF16)
    cci = _block_diag_halves(jnp.transpose(c_im.astype(F32), (0, 2, 1)), n_half).astype(BF16)
    rows = S5_TIME_CHUNK * batch
    assert batch % SUBLANES == 0 and rows_total % rows == 0 and n_state % S5_COL_BLOCK == 0
    in_specs = [
        pl.BlockSpec((rows, width), lambda i: (i, 0)),
        _const_spec((1, n_state)), _const_spec((1, n_state)),
        _const_spec(bbr.shape), _const_spec(bbi.shape), _const_spec(ccr.shape), _const_spec(cci.shape),
        _const_spec((1, width)), _const_spec(w_glu.shape),
    ]
    return pl.pallas_call(
        functools.partial(_s5_kernel, batch=batch),
        grid=(rows_total // rows,), in_specs=in_specs,
        out_specs=pl.BlockSpec((rows, width), lambda i: (i, 0)),
        out_shape=jax.ShapeDtypeStruct((rows_total, width), BF16),
        scratch_shapes=[pltpu.VMEM((rows, n_state), F32), pltpu.VMEM((rows, n_state), F32),
                        pltpu.VMEM((batch, n_state), F32), pltpu.VMEM((batch, n_state), F32)],
        compiler_params=_params(("arbitrary",)), name="s5_mixer",
    )(u_tb, abar_r.reshape(1, n_state), abar_i.reshape(1, n_state), bbr, bbi, ccr, cci,
      d_skip.astype(F32).reshape(1, width), w_glu.astype(BF16))


def _mix_kernel(o_ref, s_ref, gate_ref, x_ref, wa_ref, wb_ref, wo_ref, gf_ref, wr_ref, br_ref,
                x1_ref, h2_ref, route_ref, *, n_groups, per_group):
    d = x_ref.shape[1]
    tm = x_ref.shape[0]
    ya = _dot(o_ref[...], wa_ref[...])
    yb = _dot(s_ref[...], wb_ref[...])
    mix = gate_ref[:, :d] * ya + gate_ref[:, d:] * yb
    x1 = x_ref[...] + _dot(mix.astype(BF16), wo_ref[...])
    x1_ref[...] = x1
    h2b = _rms(x1, gf_ref[...]).astype(BF16)
    h2_ref[...] = h2b
    lt = _dot_nt(wr_ref[...], h2b) + br_ref[...]
    n_exp = n_groups * per_group
    lg = lt[n_exp:n_exp + n_groups, :]
    eg = jnp.exp(lg - jnp.max(lg, axis=0, keepdims=True))
    pg = eg / jnp.sum(eg, axis=0, keepdims=True)
    gval = jnp.max(pg, axis=0, keepdims=True)
    gi = lax.broadcasted_iota(I32, (n_groups, tm), 0)
    grp = jnp.min(jnp.where(pg == gval, gi, n_groups), axis=0, keepdims=True)
    le = jnp.zeros((per_group, tm), F32)
    for g in range(n_groups):
        le = jnp.where(grp == g, lt[per_group * g:per_group * (g + 1), :], le)
    ee = jnp.exp(le - jnp.max(le, axis=0, keepdims=True))
    pe = ee / jnp.sum(ee, axis=0, keepdims=True)
    ei = lax.broadcasted_iota(I32, (per_group, tm), 0)
    p0 = jnp.max(pe, axis=0, keepdims=True)
    i0 = jnp.min(jnp.where(pe == p0, ei, per_group), axis=0, keepdims=True)
    pe2 = jnp.where(ei == i0, -1.0, pe)
    p1 = jnp.max(pe2, axis=0, keepdims=True)
    i1 = jnp.min(jnp.where(pe2 == p1, ei, per_group), axis=0, keepdims=True)
    psum = p0 + p1
    w0 = gval * (p0 / psum)
    w1 = gval * (p1 / psum)
    e0 = (grp * per_group + i0).astype(F32)
    e1 = (grp * per_group + i1).astype(F32)
    zero = jnp.zeros((SUBLANES - 4, tm), F32)
    route_ref[...] = jnp.concatenate([e0, e1, w0, w1, zero], axis=0)


def _mix(o, s5_tm, gates, x2, w_up_a, w_up_b, w_out, g_ffn, w_rg, b_rg, w_re, b_re, B, S):
    T, D = x2.shape
    tm = ROW_TILE
    spt = S // tm
    n_s5 = s5_tm.shape[1] // B
    n_groups = w_rg.shape[1]
    n_exp = w_re.shape[1]
    per_group = n_exp // n_groups
    n_r = -(-(n_exp + n_groups) // SUBLANES) * SUBLANES
    wr = jnp.zeros((n_r, D), BF16).at[:n_exp].set(w_re.T.astype(BF16)).at[n_exp:n_exp + n_groups].set(
        w_rg.T.astype(BF16))
    br = jnp.zeros((n_r, 1), F32).at[:n_exp, 0].set(b_re.astype(F32)).at[n_exp:n_exp + n_groups, 0].set(
        b_rg.astype(F32))
    wa, wb_, wo = w_up_a.astype(BF16), w_up_b.astype(BF16), w_out.astype(BF16)
    in_specs = [
        pl.BlockSpec((tm, o.shape[1]), lambda i: (i, 0)),
        pl.BlockSpec((tm, n_s5), lambda i: (i % spt, i // spt)),
        pl.BlockSpec((tm, gates.shape[1]), lambda i: (i, 0)),
        pl.BlockSpec((tm, D), lambda i: (i, 0)),
        _const_spec(wa.shape), _const_spec(wb_.shape), _const_spec(wo.shape), _const_spec((1, D)),
        _const_spec(wr.shape), _const_spec(br.shape),
    ]
    out_shape = (jax.ShapeDtypeStruct((T, D), F32), jax.ShapeDtypeStruct((T, D), BF16),
                 jax.ShapeDtypeStruct((SUBLANES, T), F32))
    out_specs = (pl.BlockSpec((tm, D), lambda i: (i, 0)), pl.BlockSpec((tm, D), lambda i: (i, 0)),
                 pl.BlockSpec((SUBLANES, tm), lambda i: (0, i)))
    return pl.pallas_call(
        functools.partial(_mix_kernel, n_groups=n_groups, per_group=per_group),
        grid=(T // tm,), in_specs=in_specs, out_specs=out_specs, out_shape=out_shape,
        compiler_params=_params(("parallel",)), name="mix_router",
    )(o, s5_tm, gates, x2, wa, wb_, wo, g_ffn.reshape(1, D), wr, br)


def _moe_kernel(be_ref, x_ref, rw_ref, wg_ref, wu_ref, wd_ref, y_ref):
    del be_ref
    xb = x_ref[...]
    g = _dot(xb, wg_ref[...])
    hb = (g * (1.0 / (1.0 + jnp.exp(-g)))) * _dot(xb, wu_ref[...])
    y_ref[...] = _dot(hb.astype(BF16), wd_ref[...]) * rw_ref[...]


def _moe_ffn(xs, row_w, blk_exp, w_gate, w_up, w_down):
    n_rows, D = xs.shape
    E, _, De = w_gate.shape
    bm = MOE_ROWS
    grid_spec = pltpu.PrefetchScalarGridSpec(
        num_scalar_prefetch=1, grid=(n_rows // bm,),
        in_specs=[
            pl.BlockSpec((bm, D), lambda i, be: (i, 0)),
            pl.BlockSpec((bm, 1), lambda i, be: (i, 0)),
            pl.BlockSpec((None, D, De), lambda i, be: (be[i], 0, 0)),
            pl.BlockSpec((None, D, De), lambda i, be: (be[i], 0, 0)),
            pl.BlockSpec((None, De, D), lambda i, be: (be[i], 0, 0)),
        ],
        out_specs=pl.BlockSpec((bm, D), lambda i, be: (i, 0)),
    )
    return pl.pallas_call(
        _moe_kernel, grid_spec=grid_spec, out_shape=jax.ShapeDtypeStruct((n_rows, D), F32),
        compiler_params=_params(("arbitrary",)), name="moe_ffn",
    )(blk_exp, xs, row_w, w_gate.astype(BF16), w_up.astype(BF16), w_down.astype(BF16))


def _final_kernel(x1_ref, y0_ref, y1_ref, g_ref, o_ref):
    o_ref[...] = _rms(x1_ref[...] + (y0_ref[...] + y1_ref[...]), g_ref[...])


def _final(x1, y0, y1, g_final):
    T, D = x1.shape
    tm = ROW_TILE
    spec = pl.BlockSpec((tm, D), lambda i: (i, 0))
    return pl.pallas_call(
        _final_kernel, grid=(T // tm,), in_specs=[spec, spec, spec, _const_spec((1, D))], out_specs=spec,
        out_shape=jax.ShapeDtypeStruct((T, D), F32), compiler_params=_params(("parallel",)), name="final_norm",
    )(x1, y0, y1, g_final.reshape(1, D))


def _dispatch_plan(eid, weight, n_experts, bm):
    T = eid.shape[0]
    M = 2 * T
    e_flat = eid.reshape(M)
    a_idx = jnp.arange(M, dtype=I32)
    _, order = lax.sort((e_flat, a_idx), num_keys=1)
    _, rank = lax.sort((order, a_idx), num_keys=1)
    experts = jnp.arange(n_experts, dtype=I32)
    counts = jnp.sum((e_flat[:, None] == experts[None, :]).astype(I32), axis=0)
    starts = jnp.cumsum(counts) - counts
    padded = (counts + bm - 1) // bm * bm
    pad_end = jnp.cumsum(padded)
    pad_start = pad_end - padded
    n_rows = (M + n_experts * (bm - 1) + bm - 1) // bm * bm
    blk_start = jnp.arange(n_rows // bm, dtype=I32) * bm
    blk_exp = jnp.minimum(jnp.sum((pad_end[None, :] <= blk_start[:, None]).astype(I32), axis=1), n_experts - 1)
    e_row = jnp.repeat(blk_exp, bm)
    off = jnp.arange(n_rows, dtype=I32) - pad_start[e_row]
    valid = off < counts[e_row]
    a_row = order[jnp.clip(starts[e_row] + off, 0, M - 1)]
    row_tok = jnp.where(valid, a_row // 2, T)
    row_w = jnp.where(valid, weight.reshape(M)[a_row], 0.0)
    pos = pad_start[e_flat] + (rank - starts[e_flat])
    return row_tok, row_w, pos.reshape(T, 2), blk_exp


def kernel(x, g_mix, w_in, g_kv, w_uk, w_uv, rel_bias, s5_a_re, s5_a_im, s5_log_dt, s5_b_re, s5_b_im,
           s5_c_re, s5_c_im, s5_d, w_glu, w_up_a, w_up_b, b_gate, w_out, g_ffn, w_router_g, b_router_g,
           w_router_e, b_router_e, w_gate, w_up, w_down, g_final):
    B, S, D = x.shape
    T = B * S
    assert S % ROW_TILE == 0 and g_mix.shape[0] == 1
    x2 = x.reshape(T, D)
    for l in range(g_mix.shape[0]):
        qlat, c, qi, ki, wit, u_tm, gates = _inproj(x2, g_mix[l], w_in[l], g_kv[l], w_uk[l], b_gate[l], B, S)
        o = _attention(qlat, c, qi, ki, wit, w_uv[l], rel_bias, B, S).reshape(T, -1)
        u_tb = u_tm.reshape(T, -1)
        s5_tb = _s5(u_tb, s5_a_re[l], s5_a_im[l], s5_log_dt[l], s5_b_re[l], s5_b_im[l], s5_c_re[l],
                    s5_c_im[l], s5_d[l], w_glu[l], B)
        x1, h2, route = _mix(o, s5_tb.reshape(S, -1), gates, x2, w_up_a[l], w_up_b[l], w_out[l], g_ffn[l],
                             w_router_g[l], b_router_g[l], w_router_e[l], b_router_e[l], B, S)
        eid = jnp.transpose(route[0:2]).astype(I32)
        weight = jnp.transpose(route[2:4])
        n_experts = w_gate.shape[1]
        row_tok, row_w, pos, blk_exp = _dispatch_plan(eid, weight, n_experts, MOE_ROWS)
        h2_pad = jnp.concatenate([h2, jnp.zeros((1, D), h2.dtype)], axis=0)
        yb = _moe_ffn(h2_pad[row_tok], row_w[:, None], blk_exp, w_gate[l], w_up[l], w_down[l])
        x2 = x1
        y0, y1 = yb[pos[:, 0]], yb[pos[:, 1]]
    out = _final(x2, y0, y1, g_final)
    return out.reshape(B, S, D)
```

```python
import functools
import math

import jax
import jax.numpy as jnp
from jax import lax
from jax.experimental import pallas as pl
from jax.experimental.pallas import tpu as pltpu

F32 = jnp.float32
BF16 = jnp.bfloat16
I32 = jnp.int32

CHUNK = 64
TOPK_MAX = 256
IDX_SCALE = 0.0625
N_BUCKETS = 32
MAX_DISTANCE = 128
EPS = 1e-6
SPLIT_POINTS = (512, 640, 896, 960, 964, 1476)

LANES = 128
SUBLANES = 8
VMEM_LIMIT_BYTES = 56 * 1024 * 1024

ROW_TILE = 512
ATTN_TQ = 128
ATTN_TK = 256
S5_TIME_CHUNK = 64
S5_COL_BLOCK = 512
MOE_ROWS = 256

INT_MIN = -(2 ** 31)
INT16_MIN = -(2 ** 15)
PACK = 16
LOG2E = math.log2(math.e)
ONES_ROWS = 16
NEG_MASK = -1e30
M_FLOOR = -1e20


def _dot(a, b):
    return jnp.dot(a, b, preferred_element_type=F32)


def _dot_nt(a, b):
    return lax.dot_general(a, b, (((1,), (1,)), ((), ())), preferred_element_type=F32)


def _rms(x, g):
    return x * lax.rsqrt(jnp.mean(x * x, axis=-1, keepdims=True) + EPS) * g


def _params(sem):
    return pltpu.CompilerParams(dimension_semantics=sem, vmem_limit_bytes=VMEM_LIMIT_BYTES)


def _const_spec(shape):
    n = len(shape)
    return pl.BlockSpec(shape, lambda *_: (0,) * n)


def _inproj_kernel(x_ref, g_ref, wq_ref, wuk_ref, wc_ref, gkv_ref, wqi_ref, wki_ref, wwi_ref, wu_ref,
                   wgl_ref, bg_ref, qlat_ref, c_ref, qi_ref, ki_ref, wit_ref, u_ref, gate_ref, *, q_scale):
    hb = _rms(x_ref[...], g_ref[...]).astype(BF16)
    qb = _dot(hb, wq_ref[...]).astype(BF16)
    n_pair = wuk_ref.shape[0]
    for j in range(n_pair):
        ql = _dot(qb[:, LANES * j:LANES * (j + 1)], wuk_ref[j]) * q_scale
        qlat_ref[2 * j] = ql[:, :LANES].astype(BF16)
        qlat_ref[2 * j + 1] = ql[:, LANES:].astype(BF16)
    c_ref[...] = _rms(_dot(hb, wc_ref[...]), gkv_ref[...]).astype(BF16)
    for h in range(wqi_ref.shape[0]):
        qi_ref[h] = _dot(hb, wqi_ref[h]).astype(BF16)
    ki_ref[...] = _dot(hb, wki_ref[...]).astype(BF16)
    wit_ref[...] = _dot_nt(wwi_ref[...], hb) * IDX_SCALE
    u_ref[...] = _dot(hb, wu_ref[...])
    gl = _dot(hb, wgl_ref[...]) + bg_ref[...]
    gate_ref[...] = 1.0 / (1.0 + jnp.exp(-gl))


def _inproj(x2, g_mix, w_in, g_kv, w_uk, b_gate, B, S):
    T, D = x2.shape
    R, H, dh = w_uk.shape
    sp = SPLIT_POINTS
    n_ih = sp[4] - sp[3]
    idx_dim = (sp[2] - sp[1]) // n_ih
    tm = ROW_TILE
    spt = S // tm
    wb = w_in.astype(BF16)
    wq = wb[:, :sp[0]]
    wc = wb[:, sp[0]:sp[1]]
    wqi = wb[:, sp[1]:sp[2]].reshape(D, n_ih, idx_dim).transpose(1, 0, 2)
    wki = wb[:, sp[2]:sp[3]]
    wwi = jnp.zeros((SUBLANES, D), BF16).at[:n_ih].set(wb[:, sp[3]:sp[4]].T)
    wu = wb[:, sp[4]:sp[5]]
    wgl = wb[:, sp[5]:]
    wuk_t = jnp.transpose(w_uk, (1, 2, 0)).astype(BF16)
    z = jnp.zeros((dh, R), BF16)
    wuk_pairs = jnp.stack([
        jnp.concatenate([jnp.concatenate([wuk_t[2 * j], z], axis=1),
                         jnp.concatenate([z, wuk_t[2 * j + 1]], axis=1)], axis=0)
        for j in range(H // 2)])
    n_gate = wgl.shape[1]
    n_u = wu.shape[1]
    out_shape = (
        jax.ShapeDtypeStruct((H, T, R), BF16),
        jax.ShapeDtypeStruct((T, R), BF16),
        jax.ShapeDtypeStruct((n_ih, T, idx_dim), BF16),
        jax.ShapeDtypeStruct((T, idx_dim), BF16),
        jax.ShapeDtypeStruct((SUBLANES, T), F32),
        jax.ShapeDtypeStruct((S, B * n_u), F32),
        jax.ShapeDtypeStruct((T, n_gate), F32),
    )
    in_specs = [
        pl.BlockSpec((tm, D), lambda i: (i, 0)),
        _const_spec((1, D)), _const_spec(wq.shape), _const_spec(wuk_pairs.shape), _const_spec(wc.shape),
        _const_spec((1, R)), _const_spec(wqi.shape), _const_spec(wki.shape), _const_spec(wwi.shape),
        _const_spec(wu.shape), _const_spec(wgl.shape), _const_spec((1, n_gate)),
    ]
    out_specs = (
        pl.BlockSpec((H, tm, R), lambda i: (0, i, 0)),
        pl.BlockSpec((tm, R), lambda i: (i, 0)),
        pl.BlockSpec((n_ih, tm, idx_dim), lambda i: (0, i, 0)),
        pl.BlockSpec((tm, idx_dim), lambda i: (i, 0)),
        pl.BlockSpec((SUBLANES, tm), lambda i: (0, i)),
        pl.BlockSpec((tm, n_u), lambda i: (i % spt, i // spt)),
        pl.BlockSpec((tm, n_gate), lambda i: (i, 0)),
    )
    return pl.pallas_call(
        functools.partial(_inproj_kernel, q_scale=dh ** -0.5 * LOG2E),
        grid=(T // tm,), in_specs=in_specs, out_specs=out_specs, out_shape=out_shape,
        compiler_params=_params(("parallel",)), name="inproj",
    )(x2, g_mix.reshape(1, D), wq, wuk_pairs, wc, g_kv.reshape(1, R), wqi, wki, wwi, wu, wgl,
      b_gate.reshape(1, n_gate))


def _attn_kernel(qlat_ref, qi_ref, wit_ref, ki_ref, c_ref, ct_ref, bias_ref, tri_ref, eye_ref, wuv_ref, o_ref,
                 keys_ref, hi_ref, lo_ref, mask_ref, qa_ref, lga_ref, lgb_ref, tma_ref, tmb_ref, m_ref, acc_ref,
                 tie_ref, *, top_k):
    H, TQ, R = qlat_ref.shape
    n_ih = qi_ref.shape[0]
    TK = tri_ref.shape[0]
    j = pl.program_id(1)
    q0 = j * TQ
    n_kt = (q0 + TQ + TK - 1) // TK
    n_pair = (n_kt + 1) // 2
    HT = TK // 2

    lane = lax.broadcasted_iota(I32, (1, TQ), 1)
    limit = ((q0 + lane) // CHUNK + 1) * CHUNK

    qi_all = qi_ref[...].reshape(n_ih * TQ, qi_ref.shape[2])
    wt = wit_ref[...]

    def score_tile(kt):
        k0 = pl.multiple_of(kt * TK, TK)
        sc = _dot_nt(ki_ref[pl.ds(k0, TK), :], qi_all)
        s = jnp.zeros((TK, TQ), F32)
        for h in range(n_ih):
            s = s + wt[h:h + 1, :] * jnp.maximum(sc[:, h * TQ:(h + 1) * TQ], 0.0)
        s = jnp.where(s == 0.0, 0.0, s)
        bits = lax.bitcast_convert_type(s, I32)
        key = bits ^ ((bits >> 31) & 0x7FFFFFFF)
        row = k0 + lax.broadcasted_iota(I32, (TK, 1), 0)
        key = jnp.where(row < limit, key, INT_MIN)
        keys_ref[pl.ds(k0, TK), :] = key
        hi_ref[pl.ds(pl.multiple_of(kt * HT, HT), HT), :] = (key[:HT] & -65536) | ((key[HT:] >> 16) & 0xFFFF)

    def score_pair(i, carry):
        score_tile(2 * i)
        score_tile(2 * i + 1)
        return carry

    lax.fori_loop(0, n_pair, score_pair, 0)

    one16, zero16 = jnp.int16(1), jnp.int16(0)

    def count16(ref, pred):
        def body(i, acc):
            words = ref[pl.ds(pl.multiple_of(i * TK, TK), TK), :]
            ind = jnp.where(pred(pltpu.bitcast(words, jnp.int16)), one16, zero16)
            parts = [ind[PACK * r:PACK * (r + 1)] for r in range(2 * TK // PACK)]
            while len(parts) > 1:
                parts = [a + b for a, b in zip(parts[::2], parts[1::2])]
            return acc + parts[0]
        acc = lax.fori_loop(0, n_pair, body, jnp.zeros((PACK, TQ), jnp.int16)).astype(I32)
        return (acc[:SUBLANES] + acc[SUBLANES:]).sum(axis=0, keepdims=True)

    def kth_largest16(ref, k):
        def bit_body(i, lo):
            cand = lo + lax.shift_left(jnp.int32(1), 15 - i)
            c16 = cand.astype(jnp.int16)
            cnt = count16(ref, lambda blk: blk >= c16)
            return jnp.where(cnt >= k, cand, lo)
        return lax.fori_loop(0, 16, bit_body, jnp.full((1, TQ), INT16_MIN, I32))

    t_hi = kth_largest16(hi_ref, top_k)
    t_hi16 = t_hi.astype(jnp.int16)
    n_hi_gt = count16(hi_ref, lambda blk: blk > t_hi16)

    def low_tile(kt, carry):
        key = keys_ref[pl.ds(pl.multiple_of(kt * TK, TK), TK), :]
        low = jnp.where((key >> 16) == t_hi, (key ^ 0x8000) & 0xFFFF, 0x8000)
        lo_ref[pl.ds(pl.multiple_of(kt * HT, HT), HT), :] = (low[:HT] << 16) | low[HT:]
        return carry

    lax.fori_loop(0, 2 * n_pair, low_tile, 0)
    t_lo = kth_largest16(lo_ref, top_k - n_hi_gt)
    t_lo16 = t_lo.astype(jnp.int16)
    thr = (t_hi << 16) | ((t_lo ^ 0x8000) & 0xFFFF)
    n_gt = n_hi_gt + count16(lo_ref, lambda blk: blk > t_lo16)
    n_tie = jnp.where(thr == INT_MIN, 0, top_k - n_gt).astype(F32)

    tie_ref[...] = jnp.zeros(tie_ref.shape, F32)

    def mask_pair(i, carry):
        ks = [pl.multiple_of((2 * i + t) * TK, TK) for t in range(2)]
        keys = [keys_ref[pl.ds(k0, TK), :] for k0 in ks]
        eqs = [key == thr for key in keys]
        ranks = [_dot(tri_ref[...], jnp.where(eq, 1.0, 0.0).astype(BF16)) for eq in eqs]
        tie = tie_ref[...]
        for k0, key, eq, rank in zip(ks, keys, eqs, ranks):
            rank = rank + tie
            tie = rank[TK - 1:TK, :]
            negmask = jnp.where(key > thr, 0.0,
                                jnp.where(eq, jnp.where(rank <= n_tie, 0.0, NEG_MASK), NEG_MASK))
            mask_ref[pl.ds(k0, TK), :] = negmask.astype(BF16)
        tie_ref[...] = tie
        return carry

    lax.fori_loop(0, n_pair, mask_pair, 0)

    qa_ref[...] = jnp.concatenate([qlat_ref[...].reshape(H * TQ, R), eye_ref[...]], axis=1)
    m_ref[...] = jnp.full(m_ref.shape, M_FLOOR, F32)
    acc_ref[...] = jnp.zeros(acc_ref.shape, F32)

    def issue_logits(kt, lg_ref, tm_ref):
        k0 = pl.multiple_of(jnp.minimum(kt, 2 * n_pair - 1) * TK, TK)
        c_aug = jnp.concatenate([c_ref[pl.ds(k0, TK), :], mask_ref[pl.ds(k0, TK), :]], axis=1)
        logits = _dot_nt(c_aug, qa_ref[...])
        boff = pl.multiple_of(jnp.clip(k0 - q0 + 2 * TK, 0, 2 * TK), LANES)
        for h in range(H):
            sl = slice(h * TQ, (h + 1) * TQ)
            lg = logits[:, sl] + bias_ref[h, pl.ds(boff, TK), :]
            lg_ref[:, sl] = lg
            tm_ref[:, sl] = jnp.max(lg, axis=0, keepdims=True)

    def softmax_pv(kt, lg_ref, tm_ref):
        ct_tile = ct_ref[kt]
        for hp in range(H // 2):
            cols = slice(2 * hp * TQ, 2 * (hp + 1) * TQ)
            m_old = m_ref[:, cols]
            m_new = jnp.maximum(m_old, tm_ref[:, cols])
            m_ref[:, cols] = m_new
            alpha = jnp.exp2(m_old - m_new)
            p = jnp.exp2(lg_ref[:, cols] - m_new).astype(BF16)
            acc_ref[:, cols] = acc_ref[:, cols] * alpha + _dot(ct_tile, p)

    issue_logits(0, lga_ref, tma_ref)

    def flash_pair(i):
        issue_logits(2 * i + 1, lgb_ref, tmb_ref)
        softmax_pv(2 * i, lga_ref, tma_ref)
        issue_logits(2 * i + 2, lga_ref, tma_ref)
        softmax_pv(2 * i + 1, lgb_ref, tmb_ref)

    def flash_quad(i, carry):
        flash_pair(2 * i)
        flash_pair(2 * i + 1)
        return carry

    def flash_rest(i, carry):
        flash_pair(i)
        return carry

    n_quad = n_pair // 2
    lax.fori_loop(0, n_quad, flash_quad, 0)
    lax.fori_loop(2 * n_quad, n_pair, flash_rest, 0)

    olat = acc_ref[:R, :] * (1.0 / acc_ref[R:R + 1, :])
    o = jnp.zeros((TQ, wuv_ref.shape[2]), F32)
    for h in range(H):
        o = o + _dot(olat[:, h * TQ:(h + 1) * TQ].T.astype(BF16), wuv_ref[h])
    o_ref[...] = o.astype(BF16)


def _t5_bucket(rel):
    half = N_BUCKETS // 2
    max_exact = half // 2
    ret = jnp.where(rel > 0, half, 0)
    n = jnp.abs(rel)
    nf = jnp.maximum(n, 1).astype(F32)
    large = max_exact + (jnp.log(nf / max_exact) / math.log(MAX_DISTANCE / max_exact)
                         * (half - max_exact)).astype(I32)
    large = jnp.minimum(large, half - 1)
    return ret + jnp.where(n < max_exact, n, large)


def _attention(qlat, c, qi, ki, wit, w_uv, rel_bias, B, S):
    H, T, R = qlat.shape
    n_ih, _, idx_dim = qi.shape
    dh = w_uv.shape[2]
    TQ, TK = ATTN_TQ, ATTN_TK
    assert S % (2 * TK) == 0 and S % TQ == 0 and TK % TQ == 0 and MAX_DISTANCE <= TQ
    top_k = min(TOPK_MAX, S // 4)
    n_qt = S // TQ
    i = jnp.arange(3 * TK, dtype=I32)[:, None] - 2 * TK
    jq = jnp.arange(TQ, dtype=I32)[None, :]
    bucket = _t5_bucket(i - jq)[None]
    rb = (rel_bias - rel_bias[_t5_bucket(jnp.int32(-MAX_DISTANCE))][None, :]).astype(F32).T
    bias_t = jnp.zeros((H,) + bucket.shape[1:], F32)
    for k in range(N_BUCKETS):
        bias_t = jnp.where(bucket == k, rb[:, k][:, None, None], bias_t)
    bias_t = bias_t * LOG2E
    tri = (jnp.arange(TK)[:, None] >= jnp.arange(TK)[None, :]).astype(BF16)
    eye = jnp.tile(jnp.eye(TQ, dtype=BF16), (H, 1))
    wuv_h = jnp.transpose(w_uv, (1, 0, 2)).astype(BF16)
    wuv_pad = jnp.zeros((H, R, H * dh), BF16)
    for h in range(H):
        wuv_pad = wuv_pad.at[h, :, h * dh:(h + 1) * dh].set(wuv_h[h])
    c3 = c.reshape(B, S, R)
    ct = jnp.transpose(c3.reshape(B, S // TK, TK, R), (0, 1, 3, 2))
    ones_rows = jnp.zeros((B, S // TK, ONES_ROWS, TK), BF16).at[:, :, 0, :].set(1.0)
    ct = jnp.concatenate([ct, ones_rows], axis=2)
    grid = (B, n_qt)
    in_specs = [
        pl.BlockSpec((H, None, TQ, R), lambda b, j: (0, b, j, 0)),
        pl.BlockSpec((n_ih, None, TQ, idx_dim), lambda b, j: (0, b, j, 0)),
        pl.BlockSpec((SUBLANES, TQ), lambda b, j: (0, b * n_qt + j)),
        pl.BlockSpec((None, S, idx_dim), lambda b, j: (b, 0, 0)),
        pl.BlockSpec((None, S, R), lambda b, j: (b, 0, 0)),
        pl.BlockSpec((None, S // TK, R + ONES_ROWS, TK), lambda b, j: (b, 0, 0, 0)),
        _const_spec(bias_t.shape), _const_spec(tri.shape), _const_spec(eye.shape), _const_spec(wuv_pad.shape),
    ]
    return pl.pallas_call(
        functools.partial(_attn_kernel, top_k=top_k),
        grid=grid, in_specs=in_specs,
        out_specs=pl.BlockSpec((None, TQ, H * dh), lambda b, j: (b, j, 0)),
        out_shape=jax.ShapeDtypeStruct((B, S, H * dh), BF16),
        scratch_shapes=[pltpu.VMEM((S, TQ), I32), pltpu.VMEM((S // 2, TQ), I32), pltpu.VMEM((S // 2, TQ), I32),
                        pltpu.VMEM((S, TQ), BF16),
                        pltpu.VMEM((H * TQ, R + TQ), BF16), pltpu.VMEM((TK, H * TQ), F32),
                        pltpu.VMEM((TK, H * TQ), F32), pltpu.VMEM((1, H * TQ), F32), pltpu.VMEM((1, H * TQ), F32),
                        pltpu.VMEM((1, H * TQ), F32),
                        pltpu.VMEM((R + ONES_ROWS, H * TQ), F32), pltpu.VMEM((1, TQ), F32)],
        compiler_params=_params(("parallel", "arbitrary")), name="dsa_attention",
    )(qlat.reshape(H, B, S, R), qi.reshape(n_ih, B, S, idx_dim), wit, ki.reshape(B, S, idx_dim), c3, ct,
      bias_t, tri, eye, wuv_pad)


def _s5_kernel(u_ref, ar_ref, ai_ref, bbr_ref, bbi_ref, ccr_ref, cci_ref, d_ref, wglu_ref, o_ref,
               xr_ref, xi_ref, sr_ref, si_ref, *, batch):
    rows, width = u_ref.shape
    n_state = ar_ref.shape[1]
    n_half = bbr_ref.shape[0]
    uw = width // n_half
    sw = n_state // n_half
    steps = rows // batch

    @pl.when(pl.program_id(0) == 0)
    def _():
        sr_ref[...] = jnp.zeros(sr_ref.shape, F32)
        si_ref[...] = jnp.zeros(si_ref.shape, F32)

    u = u_ref[...]
    ub = u.astype(BF16)
    for j in range(n_half):
        uj = ub[:, uw * j:uw * (j + 1)]
        xr_ref[:, sw * j:sw * (j + 1)] = _dot(uj, bbr_ref[j])
        xi_ref[:, sw * j:sw * (j + 1)] = _dot(uj, bbi_ref[j])

    cb = S5_COL_BLOCK
    for k in range(n_state // cb):
        cs = slice(cb * k, cb * (k + 1))
        ar = jnp.broadcast_to(ar_ref[:, cs], (batch, cb))
        ai = jnp.broadcast_to(ai_ref[:, cs], (batch, cb))

        def step(t, carry):
            xr, xi = carry
            r0 = pl.multiple_of(t * batch, batch)
            nr = ar * xr - ai * xi + xr_ref[pl.ds(r0, batch), cs]
            ni = ar * xi + ai * xr + xi_ref[pl.ds(r0, batch), cs]
            xr_ref[pl.ds(r0, batch), cs] = nr
            xi_ref[pl.ds(r0, batch), cs] = ni
            return nr, ni

        xr, xi = lax.fori_loop(0, steps, step, (sr_ref[:, cs], si_ref[:, cs]), unroll=8)
        sr_ref[:, cs] = xr
        si_ref[:, cs] = xi

    ys = []
    for j in range(n_half):
        xs_r = xr_ref[:, sw * j:sw * (j + 1)].astype(BF16)
        xs_i = xi_ref[:, sw * j:sw * (j + 1)].astype(BF16)
        ys.append(_dot(xs_r, ccr_ref[j]) - _dot(xs_i, cci_ref[j]))
    y = jnp.concatenate(ys, axis=1) + d_ref[...] * u
    z = 0.5 * y * (1.0 + jnp.tanh(math.sqrt(2.0 / math.pi) * (y + 0.044715 * (y * y * y))))
    vg = _dot(z.astype(BF16), wglu_ref[...])
    half = vg.shape[1] // 2
    o_ref[...] = (vg[:, :half] * (1.0 / (1.0 + jnp.exp(-vg[:, half:])))).astype(BF16)


def _block_diag_halves(blocks, n_half):
    G, r, c = blocks.shape
    g2 = G // n_half
    eye = jnp.eye(g2, dtype=blocks.dtype)
    b = blocks.reshape(n_half, g2, r, c)
    return jnp.einsum('jgrc,gh->jgrhc', b, eye).reshape(n_half, g2 * r, g2 * c)


def _s5(u_tb, a_re, a_im, log_dt, b_re, b_im, c_re, c_im, d_skip, w_glu, batch):
    rows_total, width = u_tb.shape
    G, N = a_re.shape
    P = b_re.shape[2]
    n_state = G * N
    n_half = 2
    dt = jnp.exp(log_dt.astype(F32))[:, None]
    ar, ai = a_re.astype(F32), a_im.astype(F32)
    mag = jnp.exp(ar * dt)
    abar_r = mag * jnp.cos(ai * dt)
    abar_i = mag * jnp.sin(ai * dt)
    nr, ni = abar_r - 1.0, abar_i
    den = ar * ar + ai * ai
    fr = (nr * ar + ni * ai) / den
    fi = (ni * ar - nr * ai) / den
    br_, bi_ = b_re.astype(F32), b_im.astype(F32)
    bb_r = fr[..., None] * br_ - fi[..., None] * bi_
    bb_i = fr[..., None] * bi_ + fi[..., None] * br_
    bbr = _block_diag_halves(jnp.transpose(bb_r, (0, 2, 1)), n_half).astype(BF16)
    bbi = _block_diag_halves(jnp.transpose(bb_i, (0, 2, 1)), n_half).astype(BF16)
    ccr = _block_diag_halves(jnp.transpose(c_re.astype(F32), (0, 2, 1)), n_half).astype(BF16)
    cci = _block_diag_halves(jnp.transpose(c_im.astype(F32), (0, 2, 1)), n_half).astype(BF16)
    rows = S5_TIME_CHUNK * batch
    assert batch % SUBLANES == 0 and rows_total % rows == 0 and n_state % S5_COL_BLOCK == 0
    in_specs = [
        pl.BlockSpec((rows, width), lambda i: (i, 0)),
        _const_spec((1, n_state)), _const_spec((1, n_state)),
        _const_spec(bbr.shape), _const_spec(bbi.shape), _const_spec(ccr.shape), _const_spec(cci.shape),
        _const_spec((1, width)), _const_spec(w_glu.shape),
    ]
    return pl.pallas_call(
        functools.partial(_s5_kernel, batch=batch),
        grid=(rows_total // rows,), in_specs=in_specs,
        out_specs=pl.BlockSpec((rows, width), lambda i: (i, 0)),
        out_shape=jax.ShapeDtypeStruct((rows_total, width), BF16),
        scratch_shapes=[pltpu.VMEM((rows, n_state), F32), pltpu.VMEM((rows, n_state), F32),
                        pltpu.VMEM((batch, n_state), F32), pltpu.VMEM((batch, n_state), F32)],
        compiler_params=_params(("arbitrary",)), name="s5_mixer",
    )(u_tb, abar_r.reshape(1, n_state), abar_i.reshape(1, n_state), bbr, bbi, ccr, cci,
      d_skip.astype(F32).reshape(1, width), w_glu.astype(BF16))


def _mix_kernel(o_ref, s_ref, gate_ref, x_ref, wa_ref, wb_ref, wo_ref, gf_ref, wr_ref, br_ref,
                x1_ref, h2_ref, route_ref, *, n_groups, per_group):
    d = x_ref.shape[1]
    tm = x_ref.shape[0]
    ya = _dot(o_ref[...], wa_ref[...])
    yb = _dot(s_ref[...], wb_ref[...])
    mix = gate_ref[:, :d] * ya + gate_ref[:, d:] * yb
    x1 = x_ref[...] + _dot(mix.astype(BF16), wo_ref[...])
    x1_ref[...] = x1
    h2b = _rms(x1, gf_ref[...]).astype(BF16)
    h2_ref[...] = h2b
    lt = _dot_nt(wr_ref[...], h2b) + br_ref[...]
    n_exp = n_groups * per_group
    lg = lt[n_exp:n_exp + n_groups, :]
    eg = jnp.exp(lg - jnp.max(lg, axis=0, keepdims=True))
    pg = eg / jnp.sum(eg, axis=0, keepdims=True)
    gval = jnp.max(pg, axis=0, keepdims=True)
    gi = lax.broadcasted_iota(I32, (n_groups, tm), 0)
    grp = jnp.min(jnp.where(pg == gval, gi, n_groups), axis=0, keepdims=True)
    le = jnp.zeros((per_group, tm), F32)
    for g in range(n_groups):
        le = jnp.where(grp == g, lt[per_group * g:per_group * (g + 1), :], le)
    ee = jnp.exp(le - jnp.max(le, axis=0, keepdims=True))
    pe = ee / jnp.sum(ee, axis=0, keepdims=True)
    ei = lax.broadcasted_iota(I32, (per_group, tm), 0)
    p0 = jnp.max(pe, axis=0, keepdims=True)
    i0 = jnp.min(jnp.where(pe == p0, ei, per_group), axis=0, keepdims=True)
    pe2 = jnp.where(ei == i0, -1.0, pe)
    p1 = jnp.max(pe2, axis=0, keepdims=True)
    i1 = jnp.min(jnp.where(pe2 == p1, ei, per_group), axis=0, keepdims=True)
    psum = p0 + p1
    w0 = gval * (p0 / psum)
    w1 = gval * (p1 / psum)
    e0 = (grp * per_group + i0).astype(F32)
    e1 = (grp * per_group + i1).astype(F32)
    zero = jnp.zeros((SUBLANES - 4, tm), F32)
    route_ref[...] = jnp.concatenate([e0, e1, w0, w1, zero], axis=0)


def _mix(o, s5_tm, gates, x2, w_up_a, w_up_b, w_out, g_ffn, w_rg, b_rg, w_re, b_re, B, S):
    T, D = x2.shape
    tm = ROW_TILE
    spt = S // tm
    n_s5 = s5_tm.shape[1] // B
    n_groups = w_rg.shape[1]
    n_exp = w_re.shape[1]
    per_group = n_exp // n_groups
    n_r = -(-(n_exp + n_groups) // SUBLANES) * SUBLANES
    wr = jnp.zeros((n_r, D), BF16).at[:n_exp].set(w_re.T.astype(BF16)).at[n_exp:n_exp + n_groups].set(
        w_rg.T.astype(BF16))
    br = jnp.zeros((n_r, 1), F32).at[:n_exp, 0].set(b_re.astype(F32)).at[n_exp:n_exp + n_groups, 0].set(
        b_rg.astype(F32))
    wa, wb_, wo = w_up_a.astype(BF16), w_up_b.astype(BF16), w_out.astype(BF16)
    in_specs = [
        pl.BlockSpec((tm, o.shape[1]), lambda i: (i, 0)),
        pl.BlockSpec((tm, n_s5), lambda i: (i % spt, i // spt)),
        pl.BlockSpec((tm, gates.shape[1]), lambda i: (i, 0)),
        pl.BlockSpec((tm, D), lambda i: (i, 0)),
        _const_spec(wa.shape), _const_spec(wb_.shape), _const_spec(wo.shape), _const_spec((1, D)),
        _const_spec(wr.shape), _const_spec(br.shape),
    ]
    out_shape = (jax.ShapeDtypeStruct((T, D), F32), jax.ShapeDtypeStruct((T, D), BF16),
                 jax.ShapeDtypeStruct((SUBLANES, T), F32))
    out_specs = (pl.BlockSpec((tm, D), lambda i: (i, 0)), pl.BlockSpec((tm, D), lambda i: (i, 0)),
                 pl.BlockSpec((SUBLANES, tm), lambda i: (0, i)))
    return pl.pallas_call(
        functools.partial(_mix_kernel, n_groups=n_groups, per_group=per_group),
        grid=(T // tm,), in_specs=in_specs, out_specs=out_specs, out_shape=out_shape,
        compiler_params=_params(("parallel",)), name="mix_router",
    )(o, s5_tm, gates, x2, wa, wb_, wo, g_ffn.reshape(1, D), wr, br)


def _moe_kernel(be_ref, x_ref, rw_ref, wg_ref, wu_ref, wd_ref, y_ref):
    del be_ref
    xb = x_ref[...]
    g = _dot(xb, wg_ref[...])
    hb = (g * (1.0 / (1.0 + jnp.exp(-g)))) * _dot(xb, wu_ref[...])
    y_ref[...] = _dot(hb.astype(BF16), wd_ref[...]) * rw_ref[...]


def _moe_ffn(xs, row_w, blk_exp, w_gate, w_up, w_down):
    n_rows, D = xs.shape
    E, _, De = w_gate.shape
    bm = MOE_ROWS
    grid_spec = pltpu.PrefetchScalarGridSpec(
        num_scalar_prefetch=1, grid=(n_rows // bm,),
        in_specs=[
            pl.BlockSpec((bm, D), lambda i, be: (i, 0)),
            pl.BlockSpec((bm, 1), lambda i, be: (i, 0)),
            pl.BlockSpec((None, D, De), lambda i, be: (be[i], 0, 0)),
            pl.BlockSpec((None, D, De), lambda i, be: (be[i], 0, 0)),
            pl.BlockSpec((None, De, D), lambda i, be: (be[i], 0, 0)),
        ],
        out_specs=pl.BlockSpec((bm, D), lambda i, be: (i, 0)),
    )
    return pl.pallas_call(
        _moe_kernel, grid_spec=grid_spec, out_shape=jax.ShapeDtypeStruct((n_rows, D), F32),
        compiler_params=_params(("arbitrary",)), name="moe_ffn",
    )(blk_exp, xs, row_w, w_gate.astype(BF16), w_up.astype(BF16), w_down.astype(BF16))


def _final_kernel(x1_ref, y0_ref, y1_ref, g_ref, o_ref):
    o_ref[...] = _rms(x1_ref[...] + (y0_ref[...] + y1_ref[...]), g_ref[...])


def _final(x1, y0, y1, g_final):
    T, D = x1.shape
    tm = ROW_TILE
    spec = pl.BlockSpec((tm, D), lambda i: (i, 0))
    return pl.pallas_call(
        _final_kernel, grid=(T // tm,), in_specs=[spec, spec, spec, _const_spec((1, D))], out_specs=spec,
        out_shape=jax.ShapeDtypeStruct((T, D), F32), compiler_params=_params(("parallel",)), name="final_norm",
    )(x1, y0, y1, g_final.reshape(1, D))


def _dispatch_plan(eid, weight, n_experts, bm):
    T = eid.shape[0]
    M = 2 * T
    e_flat = eid.reshape(M)
    a_idx = jnp.arange(M, dtype=I32)
    _, order = lax.sort((e_flat, a_idx), num_keys=1)
    _, rank = lax.sort((order, a_idx), num_keys=1)
    experts = jnp.arange(n_experts, dtype=I32)
    counts = jnp.sum((e_flat[:, None] == experts[None, :]).astype(I32), axis=0)
    starts = jnp.cumsum(counts) - counts
    padded = (counts + bm - 1) // bm * bm
    pad_end = jnp.cumsum(padded)
    pad_start = pad_end - padded
    n_rows = (M + n_experts * (bm - 1) + bm - 1) // bm * bm
    blk_start = jnp.arange(n_rows // bm, dtype=I32) * bm
    blk_exp = jnp.minimum(jnp.sum((pad_end[None, :] <= blk_start[:, None]).astype(I32), axis=1), n_experts - 1)
    off = (blk_start - pad_start[blk_exp])[:, None] + jnp.arange(bm, dtype=I32)[None, :]
    valid = (off < counts[blk_exp][:, None]).reshape(n_rows)
    src = jnp.clip(starts[blk_exp][:, None] + off, 0, M - 1).reshape(n_rows)
    a_row = order[src]
    row_tok = jnp.where(valid, a_row // 2, 0)
    row_w = jnp.where(valid, weight.reshape(M)[a_row], 0.0)
    pos = pad_start[e_flat] + (rank - starts[e_flat])
    return row_tok, row_w, pos.reshape(T, 2), blk_exp


def kernel(x, g_mix, w_in, g_kv, w_uk, w_uv, rel_bias, s5_a_re, s5_a_im, s5_log_dt, s5_b_re, s5_b_im,
           s5_c_re, s5_c_im, s5_d, w_glu, w_up_a, w_up_b, b_gate, w_out, g_ffn, w_router_g, b_router_g,
           w_router_e, b_router_e, w_gate, w_up, w_down, g_final):
    B, S, D = x.shape
    T = B * S
    assert S % ROW_TILE == 0 and g_mix.shape[0] == 1
    x2 = x.reshape(T, D)
    for l in range(g_mix.shape[0]):
        qlat, c, qi, ki, wit, u_tm, gates = _inproj(x2, g_mix[l], w_in[l], g_kv[l], w_uk[l], b_gate[l], B, S)
        o = _attention(qlat, c, qi, ki, wit, w_uv[l], rel_bias, B, S).reshape(T, -1)
        u_tb = u_tm.reshape(T, -1)
        s5_tb = _s5(u_tb, s5_a_re[l], s5_a_im[l], s5_log_dt[l], s5_b_re[l], s5_b_im[l], s5_c_re[l],
                    s5_c_im[l], s5_d[l], w_glu[l], B)
        x1, h2, route = _mix(o, s5_tb.reshape(S, -1), gates, x2, w_up_a[l], w_up_b[l], w_out[l], g_ffn[l],
                             w_router_g[l], b_router_g[l], w_router_e[l], b_router_e[l], B, S)
        eid = jnp.transpose(route[0:2]).astype(I32)
        weight = jnp.transpose(route[2:4])
        n_experts = w_gate.shape[1]
        row_tok, row_w, pos, blk_exp = _dispatch_plan(eid, weight, n_experts, MOE_ROWS)
        yb = _moe_ffn(h2[row_tok], row_w[:, None], blk_exp, w_gate[l], w_up[l], w_down[l])
        x2 = x1
        y0, y1 = yb[pos[:, 0]], yb[pos[:, 1]]
    out = _final(x2, y0, y1, g_final)
    return out.reshape(B, S, D)
```

```python
import functools
import math

import jax
import jax.numpy as jnp
from jax import lax
from jax.experimental import pallas as pl
from jax.experimental.pallas import tpu as pltpu

F32 = jnp.float32
BF16 = jnp.bfloat16
I32 = jnp.int32

CHUNK = 64
TOPK_MAX = 256
IDX_SCALE = 0.0625
N_BUCKETS = 32
MAX_DISTANCE = 128
EPS = 1e-6
SPLIT_POINTS = (512, 640, 896, 960, 964, 1476)

LANES = 128
SUBLANES = 8
VMEM_LIMIT_BYTES = 56 * 1024 * 1024

ROW_TILE = 512
ATTN_TQ = 128
ATTN_TK = 256
S5_TIME_CHUNK = 64
S5_COL_BLOCK = 512
MOE_ROWS = 256

INT_MIN = -(2 ** 31)
INT16_MIN = -(2 ** 15)
PACK = 16
LOG2E = math.log2(math.e)
ONES_ROWS = 16
NEG_MASK = -1e30
M_FLOOR = -1e20


def _dot(a, b):
    return jnp.dot(a, b, preferred_element_type=F32)


def _dot_nt(a, b):
    return lax.dot_general(a, b, (((1,), (1,)), ((), ())), preferred_element_type=F32)


def _rms(x, g):
    return x * lax.rsqrt(jnp.mean(x * x, axis=-1, keepdims=True) + EPS) * g


def _params(sem):
    return pltpu.CompilerParams(dimension_semantics=sem, vmem_limit_bytes=VMEM_LIMIT_BYTES)


def _const_spec(shape):
    n = len(shape)
    return pl.BlockSpec(shape, lambda *_: (0,) * n)


def _inproj_kernel(x_ref, g_ref, wq_ref, wuk_ref, wc_ref, gkv_ref, wqi_ref, wki_ref, wwi_ref, wu_ref,
                   wgl_ref, bg_ref, qlat_ref, c_ref, qi_ref, ki_ref, wit_ref, u_ref, gate_ref, *, q_scale):
    hb = _rms(x_ref[...], g_ref[...]).astype(BF16)
    qb = _dot(hb, wq_ref[...]).astype(BF16)
    n_pair = wuk_ref.shape[0]
    for j in range(n_pair):
        ql = _dot(qb[:, LANES * j:LANES * (j + 1)], wuk_ref[j]) * q_scale
        qlat_ref[2 * j] = ql[:, :LANES].astype(BF16)
        qlat_ref[2 * j + 1] = ql[:, LANES:].astype(BF16)
    c_ref[...] = _rms(_dot(hb, wc_ref[...]), gkv_ref[...]).astype(BF16)
    for h in range(wqi_ref.shape[0]):
        qi_ref[h] = _dot(hb, wqi_ref[h]).astype(BF16)
    ki_ref[...] = _dot(hb, wki_ref[...]).astype(BF16)
    wit_ref[...] = _dot_nt(wwi_ref[...], hb) * IDX_SCALE
    u_ref[...] = _dot(hb, wu_ref[...])
    gl = _dot(hb, wgl_ref[...]) + bg_ref[...]
    gate_ref[...] = (1.0 / (1.0 + jnp.exp(-gl))).astype(BF16)


def _inproj(x2, g_mix, w_in, g_kv, w_uk, b_gate):
    T, D = x2.shape
    R, H, dh = w_uk.shape
    sp = SPLIT_POINTS
    n_ih = sp[4] - sp[3]
    idx_dim = (sp[2] - sp[1]) // n_ih
    tm = ROW_TILE
    wb = w_in.astype(BF16)
    wq = wb[:, :sp[0]]
    wc = wb[:, sp[0]:sp[1]]
    wqi = wb[:, sp[1]:sp[2]].reshape(D, n_ih, idx_dim).transpose(1, 0, 2)
    wki = wb[:, sp[2]:sp[3]]
    wwi = jnp.zeros((SUBLANES, D), BF16).at[:n_ih].set(wb[:, sp[3]:sp[4]].T)
    wu = wb[:, sp[4]:sp[5]]
    wgl = wb[:, sp[5]:]
    wuk_t = jnp.transpose(w_uk, (1, 2, 0)).astype(BF16)
    z = jnp.zeros((dh, R), BF16)
    wuk_pairs = jnp.stack([
        jnp.concatenate([jnp.concatenate([wuk_t[2 * j], z], axis=1),
                         jnp.concatenate([z, wuk_t[2 * j + 1]], axis=1)], axis=0)
        for j in range(H // 2)])
    n_gate = wgl.shape[1]
    n_u = wu.shape[1]
    out_shape = (
        jax.ShapeDtypeStruct((H, T, R), BF16),
        jax.ShapeDtypeStruct((T, R), BF16),
        jax.ShapeDtypeStruct((n_ih, T, idx_dim), BF16),
        jax.ShapeDtypeStruct((T, idx_dim), BF16),
        jax.ShapeDtypeStruct((SUBLANES, T), F32),
        jax.ShapeDtypeStruct((T, n_u), F32),
        jax.ShapeDtypeStruct((T, n_gate), BF16),
    )
    in_specs = [
        pl.BlockSpec((tm, D), lambda i: (i, 0)),
        _const_spec((1, D)), _const_spec(wq.shape), _const_spec(wuk_pairs.shape), _const_spec(wc.shape),
        _const_spec((1, R)), _const_spec(wqi.shape), _const_spec(wki.shape), _const_spec(wwi.shape),
        _const_spec(wu.shape), _const_spec(wgl.shape), _const_spec((1, n_gate)),
    ]
    out_specs = (
        pl.BlockSpec((H, tm, R), lambda i: (0, i, 0)),
        pl.BlockSpec((tm, R), lambda i: (i, 0)),
        pl.BlockSpec((n_ih, tm, idx_dim), lambda i: (0, i, 0)),
        pl.BlockSpec((tm, idx_dim), lambda i: (i, 0)),
        pl.BlockSpec((SUBLANES, tm), lambda i: (0, i)),
        pl.BlockSpec((tm, n_u), lambda i: (i, 0)),
        pl.BlockSpec((tm, n_gate), lambda i: (i, 0)),
    )
    return pl.pallas_call(
        functools.partial(_inproj_kernel, q_scale=dh ** -0.5 * LOG2E),
        grid=(T // tm,), in_specs=in_specs, out_specs=out_specs, out_shape=out_shape,
        compiler_params=_params(("parallel",)), name="inproj",
    )(x2, g_mix.reshape(1, D), wq, wuk_pairs, wc, g_kv.reshape(1, R), wqi, wki, wwi, wu, wgl,
      b_gate.reshape(1, n_gate))


def _attn_kernel(qlat_ref, qi_ref, wit_ref, ki_ref, c_ref, ct_ref, bias_ref, tri_ref, eye_ref, wuv_ref, o_ref,
                 keys_ref, hi_ref, lo_ref, mask_ref, qa_ref, lga_ref, lgb_ref, tma_ref, tmb_ref, m_ref, acc_ref,
                 tie_ref, *, top_k):
    H, TQ, R = qlat_ref.shape
    n_ih = qi_ref.shape[0]
    TK = tri_ref.shape[0]
    j = pl.program_id(1)
    q0 = j * TQ
    n_kt = (q0 + TQ + TK - 1) // TK
    n_pair = (n_kt + 1) // 2
    n_quad = n_pair // 2
    HT = TK // 2

    lane = lax.broadcasted_iota(I32, (1, TQ), 1)
    limit = ((q0 + lane) // CHUNK + 1) * CHUNK

    qi_all = qi_ref[...].reshape(n_ih * TQ, qi_ref.shape[2])
    wt = wit_ref[...]

    def score_tile(kt):
        k0 = pl.multiple_of(kt * TK, TK)
        sc = _dot_nt(ki_ref[pl.ds(k0, TK), :], qi_all)
        s = jnp.zeros((TK, TQ), F32)
        for h in range(n_ih):
            s = s + wt[h:h + 1, :] * jnp.maximum(sc[:, h * TQ:(h + 1) * TQ], 0.0)
        s = jnp.where(s == 0.0, 0.0, s)
        bits = lax.bitcast_convert_type(s, I32)
        key = bits ^ ((bits >> 31) & 0x7FFFFFFF)
        row = k0 + lax.broadcasted_iota(I32, (TK, 1), 0)
        key = jnp.where(row < limit, key, INT_MIN)
        keys_ref[pl.ds(k0, TK), :] = key
        hi_ref[pl.ds(pl.multiple_of(kt * HT, HT), HT), :] = (key[:HT] & -65536) | ((key[HT:] >> 16) & 0xFFFF)

    def score_quad(i, carry):
        for t in range(4):
            score_tile(4 * i + t)
        return carry

    def score_rest(i, carry):
        score_tile(2 * i)
        score_tile(2 * i + 1)
        return carry

    lax.fori_loop(0, n_quad, score_quad, 0)
    lax.fori_loop(2 * n_quad, n_pair, score_rest, 0)

    one16, zero16 = jnp.int16(1), jnp.int16(0)

    def count16(ref, pred):
        def step(pair, n_pairs, acc):
            words = ref[pl.ds(pl.multiple_of(pair * TK, TK), n_pairs * TK), :]
            ind = jnp.where(pred(pltpu.bitcast(words, jnp.int16)), one16, zero16)
            parts = [ind[PACK * r:PACK * (r + 1)] for r in range(n_pairs * 2 * TK // PACK)]
            while len(parts) > 1:
                parts = [a + b for a, b in zip(parts[::2], parts[1::2])]
            return acc + parts[0]
        acc = lax.fori_loop(0, n_quad, lambda i, a: step(2 * i, 2, a), jnp.zeros((PACK, TQ), jnp.int16))
        acc = lax.fori_loop(2 * n_quad, n_pair, lambda i, a: step(i, 1, a), acc).astype(I32)
        return (acc[:SUBLANES] + acc[SUBLANES:]).sum(axis=0, keepdims=True)

    def kth_largest16(ref, k):
        def bit_body(i, lo):
            cand = lo + lax.shift_left(jnp.int32(1), 15 - i)
            c16 = cand.astype(jnp.int16)
            cnt = count16(ref, lambda blk: blk >= c16)
            return jnp.where(cnt >= k, cand, lo)
        return lax.fori_loop(0, 16, bit_body, jnp.full((1, TQ), INT16_MIN, I32))

    t_hi = kth_largest16(hi_ref, top_k)
    t_hi16 = t_hi.astype(jnp.int16)
    n_hi_gt = count16(hi_ref, lambda blk: blk > t_hi16)

    def low_tile(kt, carry):
        key = keys_ref[pl.ds(pl.multiple_of(kt * TK, TK), TK), :]
        low = jnp.where((key >> 16) == t_hi, (key ^ 0x8000) & 0xFFFF, 0x8000)
        lo_ref[pl.ds(pl.multiple_of(kt * HT, HT), HT), :] = (low[:HT] << 16) | low[HT:]
        return carry

    lax.fori_loop(0, 2 * n_pair, low_tile, 0)
    t_lo = kth_largest16(lo_ref, top_k - n_hi_gt)
    t_lo16 = t_lo.astype(jnp.int16)
    thr = (t_hi << 16) | ((t_lo ^ 0x8000) & 0xFFFF)
    n_gt = n_hi_gt + count16(lo_ref, lambda blk: blk > t_lo16)
    n_tie = jnp.where(thr == INT_MIN, 0, top_k - n_gt).astype(F32)

    tie_ref[...] = jnp.zeros(tie_ref.shape, F32)

    def mask_tiles(first, n_tiles):
        ks = [pl.multiple_of((first + t) * TK, TK) for t in range(n_tiles)]
        keys = [keys_ref[pl.ds(k0, TK), :] for k0 in ks]
        eqs = [key == thr for key in keys]
        ranks = [_dot(tri_ref[...], jnp.where(eq, 1.0, 0.0).astype(BF16)) for eq in eqs]
        tie = tie_ref[...]
        for k0, key, eq, rank in zip(ks, keys, eqs, ranks):
            rank = rank + tie
            tie = rank[TK - 1:TK, :]
            negmask = jnp.where(key > thr, 0.0,
                                jnp.where(eq, jnp.where(rank <= n_tie, 0.0, NEG_MASK), NEG_MASK))
            mask_ref[pl.ds(k0, TK), :] = negmask.astype(BF16)
        tie_ref[...] = tie

    def mask_quad(i, carry):
        mask_tiles(4 * i, 4)
        return carry

    def mask_rest(i, carry):
        mask_tiles(2 * i, 2)
        return carry

    lax.fori_loop(0, n_quad, mask_quad, 0)
    lax.fori_loop(2 * n_quad, n_pair, mask_rest, 0)

    qa_ref[...] = jnp.concatenate([qlat_ref[...].reshape(H * TQ, R), eye_ref[...]], axis=1)
    m_ref[...] = jnp.full(m_ref.shape, M_FLOOR, F32)
    acc_ref[...] = jnp.zeros(acc_ref.shape, F32)

    def issue_logits(kt, lg_ref, tm_ref):
        k0 = pl.multiple_of(jnp.minimum(kt, 2 * n_pair - 1) * TK, TK)
        c_aug = jnp.concatenate([c_ref[pl.ds(k0, TK), :], mask_ref[pl.ds(k0, TK), :]], axis=1)
        logits = _dot_nt(c_aug, qa_ref[...])
        boff = pl.multiple_of(jnp.clip(k0 - q0 + 2 * TK, 0, 2 * TK), LANES)
        for h in range(H):
            sl = slice(h * TQ, (h + 1) * TQ)
            lg = logits[:, sl] + bias_ref[h, pl.ds(boff, TK), :]
            lg_ref[:, sl] = lg
            tm_ref[:, sl] = jnp.max(lg, axis=0, keepdims=True)

    def softmax_pv(kt, lg_ref, tm_ref):
        ct_tile = ct_ref[kt]
        for hp in range(H // 2):
            cols = slice(2 * hp * TQ, 2 * (hp + 1) * TQ)
            m_old = m_ref[:, cols]
            m_new = jnp.maximum(m_old, tm_ref[:, cols])
            m_ref[:, cols] = m_new
            alpha = jnp.exp2(m_old - m_new)
            p = jnp.exp2(lg_ref[:, cols] - m_new).astype(BF16)
            acc_ref[:, cols] = acc_ref[:, cols] * alpha + _dot(ct_tile, p)

    issue_logits(0, lga_ref, tma_ref)

    def flash_pair(i):
        issue_logits(2 * i + 1, lgb_ref, tmb_ref)
        softmax_pv(2 * i, lga_ref, tma_ref)
        issue_logits(2 * i + 2, lga_ref, tma_ref)
        softmax_pv(2 * i + 1, lgb_ref, tmb_ref)

    def flash_quad(i, carry):
        flash_pair(2 * i)
        flash_pair(2 * i + 1)
        return carry

    def flash_rest(i, carry):
        flash_pair(i)
        return carry

    lax.fori_loop(0, n_quad, flash_quad, 0)
    lax.fori_loop(2 * n_quad, n_pair, flash_rest, 0)

    olat = acc_ref[:R, :] * (1.0 / acc_ref[R:R + 1, :])
    o = jnp.zeros((TQ, wuv_ref.shape[2]), F32)
    for h in range(H):
        o = o + _dot(olat[:, h * TQ:(h + 1) * TQ].T.astype(BF16), wuv_ref[h])
    o_ref[...] = o.astype(BF16)


def _t5_bucket(rel):
    half = N_BUCKETS // 2
    max_exact = half // 2
    ret = jnp.where(rel > 0, half, 0)
    n = jnp.abs(rel)
    nf = jnp.maximum(n, 1).astype(F32)
    large = max_exact + (jnp.log(nf / max_exact) / math.log(MAX_DISTANCE / max_exact)
                         * (half - max_exact)).astype(I32)
    large = jnp.minimum(large, half - 1)
    return ret + jnp.where(n < max_exact, n, large)


def _attention(qlat, c, qi, ki, wit, w_uv, rel_bias, B, S):
    H, T, R = qlat.shape
    n_ih, _, idx_dim = qi.shape
    dh = w_uv.shape[2]
    TQ, TK = ATTN_TQ, ATTN_TK
    assert S % (2 * TK) == 0 and S % TQ == 0 and TK % TQ == 0 and MAX_DISTANCE <= TQ
    top_k = min(TOPK_MAX, S // 4)
    n_qt = S // TQ
    i = jnp.arange(3 * TK, dtype=I32)[:, None] - 2 * TK
    jq = jnp.arange(TQ, dtype=I32)[None, :]
    bucket = _t5_bucket(i - jq)[None]
    rb = (rel_bias - rel_bias[_t5_bucket(jnp.int32(-MAX_DISTANCE))][None, :]).astype(F32).T
    bias_t = jnp.zeros((H,) + bucket.shape[1:], F32)
    for k in range(N_BUCKETS):
        bias_t = jnp.where(bucket == k, rb[:, k][:, None, None], bias_t)
    bias_t = bias_t * LOG2E
    tri = (jnp.arange(TK)[:, None] >= jnp.arange(TK)[None, :]).astype(BF16)
    eye = jnp.tile(jnp.eye(TQ, dtype=BF16), (H, 1))
    wuv_h = jnp.transpose(w_uv, (1, 0, 2)).astype(BF16)
    wuv_pad = jnp.zeros((H, R, H * dh), BF16)
    for h in range(H):
        wuv_pad = wuv_pad.at[h, :, h * dh:(h + 1) * dh].set(wuv_h[h])
    c3 = c.reshape(B, S, R)
    ct = jnp.transpose(c3.reshape(B, S // TK, TK, R), (0, 1, 3, 2))
    ones_rows = jnp.zeros((B, S // TK, ONES_ROWS, TK), BF16).at[:, :, 0, :].set(1.0)
    ct = jnp.concatenate([ct, ones_rows], axis=2)
    grid = (B, n_qt)
    in_specs = [
        pl.BlockSpec((H, None, TQ, R), lambda b, j: (0, b, j, 0)),
        pl.BlockSpec((n_ih, None, TQ, idx_dim), lambda b, j: (0, b, j, 0)),
        pl.BlockSpec((SUBLANES, TQ), lambda b, j: (0, b * n_qt + j)),
        pl.BlockSpec((None, S, idx_dim), lambda b, j: (b, 0, 0)),
        pl.BlockSpec((None, S, R), lambda b, j: (b, 0, 0)),
        pl.BlockSpec((None, S // TK, R + ONES_ROWS, TK), lambda b, j: (b, 0, 0, 0)),
        _const_spec(bias_t.shape), _const_spec(tri.shape), _const_spec(eye.shape), _const_spec(wuv_pad.shape),
    ]
    return pl.pallas_call(
        functools.partial(_attn_kernel, top_k=top_k),
        grid=grid, in_specs=in_specs,
        out_specs=pl.BlockSpec((None, TQ, H * dh), lambda b, j: (b, j, 0)),
        out_shape=jax.ShapeDtypeStruct((B, S, H * dh), BF16),
        scratch_shapes=[pltpu.VMEM((S, TQ), I32), pltpu.VMEM((S // 2, TQ), I32), pltpu.VMEM((S // 2, TQ), I32),
                        pltpu.VMEM((S, TQ), BF16),
                        pltpu.VMEM((H * TQ, R + TQ), BF16), pltpu.VMEM((TK, H * TQ), F32),
                        pltpu.VMEM((TK, H * TQ), F32), pltpu.VMEM((1, H * TQ), F32), pltpu.VMEM((1, H * TQ), F32),
                        pltpu.VMEM((1, H * TQ), F32),
                        pltpu.VMEM((R + ONES_ROWS, H * TQ), F32), pltpu.VMEM((1, TQ), F32)],
        compiler_params=_params(("parallel", "arbitrary")), name="dsa_attention",
    )(qlat.reshape(H, B, S, R), qi.reshape(n_ih, B, S, idx_dim), wit, ki.reshape(B, S, idx_dim), c3, ct,
      bias_t, tri, eye, wuv_pad)


def _s5_kernel(u_ref, ar_ref, ai_ref, bbr_ref, bbi_ref, ccr_ref, cci_ref, d_ref, wglu_ref, o_ref,
               xr_ref, xi_ref, sr_ref, si_ref):
    batch, steps, width = u_ref.shape
    rows = batch * steps
    n_state = ar_ref.shape[1]
    n_half = bbr_ref.shape[0]
    uw = width // n_half
    sw = n_state // n_half

    @pl.when(pl.program_id(0) == 0)
    def _():
        sr_ref[...] = jnp.zeros(sr_ref.shape, F32)
        si_ref[...] = jnp.zeros(si_ref.shape, F32)

    u = jnp.swapaxes(u_ref[...], 0, 1).reshape(rows, width)
    ub = u.astype(BF16)
    for j in range(n_half):
        uj = ub[:, uw * j:uw * (j + 1)]
        xr_ref[:, sw * j:sw * (j + 1)] = _dot(uj, bbr_ref[j])
        xi_ref[:, sw * j:sw * (j + 1)] = _dot(uj, bbi_ref[j])

    cb = S5_COL_BLOCK
    for k in range(n_state // cb):
        cs = slice(cb * k, cb * (k + 1))
        ar = jnp.broadcast_to(ar_ref[:, cs], (batch, cb))
        ai = jnp.broadcast_to(ai_ref[:, cs], (batch, cb))

        def step(t, carry):
            xr, xi = carry
            r0 = pl.multiple_of(t * batch, batch)
            nr = ar * xr - ai * xi + xr_ref[pl.ds(r0, batch), cs]
            ni = ar * xi + ai * xr + xi_ref[pl.ds(r0, batch), cs]
            xr_ref[pl.ds(r0, batch), cs] = nr
            xi_ref[pl.ds(r0, batch), cs] = ni
            return nr, ni

        xr, xi = lax.fori_loop(0, steps, step, (sr_ref[:, cs], si_ref[:, cs]), unroll=8)
        sr_ref[:, cs] = xr
        si_ref[:, cs] = xi

    ys = []
    for j in range(n_half):
        xs_r = xr_ref[:, sw * j:sw * (j + 1)].astype(BF16)
        xs_i = xi_ref[:, sw * j:sw * (j + 1)].astype(BF16)
        ys.append(_dot(xs_r, ccr_ref[j]) - _dot(xs_i, cci_ref[j]))
    y = jnp.concatenate(ys, axis=1) + d_ref[...] * u
    z = 0.5 * y * (1.0 + jnp.tanh(math.sqrt(2.0 / math.pi) * (y + 0.044715 * (y * y * y))))
    vg = _dot(z.astype(BF16), wglu_ref[...])
    half = vg.shape[1] // 2
    out = vg[:, :half] * (1.0 / (1.0 + jnp.exp(-vg[:, half:])))
    o_ref[...] = jnp.swapaxes(out.reshape(steps, batch, half), 0, 1).astype(BF16)


def _block_diag_halves(blocks, n_half):
    G, r, c = blocks.shape
    g2 = G // n_half
    eye = jnp.eye(g2, dtype=blocks.dtype)
    b = blocks.reshape(n_half, g2, r, c)
    return jnp.einsum('jgrc,gh->jgrhc', b, eye).reshape(n_half, g2 * r, g2 * c)


def _s5(u3, a_re, a_im, log_dt, b_re, b_im, c_re, c_im, d_skip, w_glu):
    batch, seq, width = u3.shape
    G, N = a_re.shape
    P = b_re.shape[2]
    n_state = G * N
    n_half = 2
    dt = jnp.exp(log_dt.astype(F32))[:, None]
    ar, ai = a_re.astype(F32), a_im.astype(F32)
    mag = jnp.exp(ar * dt)
    abar_r = mag * jnp.cos(ai * dt)
    abar_i = mag * jnp.sin(ai * dt)
    nr, ni = abar_r - 1.0, abar_i
    den = ar * ar + ai * ai
    fr = (nr * ar + ni * ai) / den
    fi = (ni * ar - nr * ai) / den
    br_, bi_ = b_re.astype(F32), b_im.astype(F32)
    bb_r = fr[..., None] * br_ - fi[..., None] * bi_
    bb_i = fr[..., None] * bi_ + fi[..., None] * br_
    bbr = _block_diag_halves(jnp.transpose(bb_r, (0, 2, 1)), n_half).astype(BF16)
    bbi = _block_diag_halves(jnp.transpose(bb_i, (0, 2, 1)), n_half).astype(BF16)
    ccr = _block_diag_halves(jnp.transpose(c_re.astype(F32), (0, 2, 1)), n_half).astype(BF16)
    cci = _block_diag_halves(jnp.transpose(c_im.astype(F32), (0, 2, 1)), n_half).astype(BF16)
    tc = S5_TIME_CHUNK
    rows = tc * batch
    n_out = w_glu.shape[1] // 2
    assert batch % SUBLANES == 0 and seq % tc == 0 and n_state % S5_COL_BLOCK == 0
    in_specs = [
        pl.BlockSpec((batch, tc, width), lambda i: (0, i, 0)),
        _const_spec((1, n_state)), _const_spec((1, n_state)),
        _const_spec(bbr.shape), _const_spec(bbi.shape), _const_spec(ccr.shape), _const_spec(cci.shape),
        _const_spec((1, width)), _const_spec(w_glu.shape),
    ]
    return pl.pallas_call(
        _s5_kernel, grid=(seq // tc,), in_specs=in_specs,
        out_specs=pl.BlockSpec((batch, tc, n_out), lambda i: (0, i, 0)),
        out_shape=jax.ShapeDtypeStruct((batch, seq, n_out), BF16),
        scratch_shapes=[pltpu.VMEM((rows, n_state), F32), pltpu.VMEM((rows, n_state), F32),
                        pltpu.VMEM((batch, n_state), F32), pltpu.VMEM((batch, n_state), F32)],
        compiler_params=_params(("arbitrary",)), name="s5_mixer",
    )(u3, abar_r.reshape(1, n_state), abar_i.reshape(1, n_state), bbr, bbi, ccr, cci,
      d_skip.astype(F32).reshape(1, width), w_glu.astype(BF16))


def _mix_kernel(o_ref, s_ref, gate_ref, x_ref, wa_ref, wb_ref, wo_ref, gf_ref, wr_ref, br_ref,
                x1_ref, h2_ref, route_ref, *, n_groups, per_group):
    d = x_ref.shape[1]
    tm = x_ref.shape[0]
    ya = _dot(o_ref[...], wa_ref[...])
    yb = _dot(s_ref[...], wb_ref[...])
    mix = gate_ref[:, :d].astype(F32) * ya + gate_ref[:, d:].astype(F32) * yb
    x1 = x_ref[...] + _dot(mix.astype(BF16), wo_ref[...])
    x1_ref[...] = x1
    h2b = _rms(x1, gf_ref[...]).astype(BF16)
    h2_ref[...] = h2b
    lt = _dot_nt(wr_ref[...], h2b) + br_ref[...]
    n_exp = n_groups * per_group
    lg = lt[n_exp:n_exp + n_groups, :]
    eg = jnp.exp(lg - jnp.max(lg, axis=0, keepdims=True))
    pg = eg / jnp.sum(eg, axis=0, keepdims=True)
    gval = jnp.max(pg, axis=0, keepdims=True)
    gi = lax.broadcasted_iota(I32, (n_groups, tm), 0)
    grp = jnp.min(jnp.where(pg == gval, gi, n_groups), axis=0, keepdims=True)
    le = jnp.zeros((per_group, tm), F32)
    for g in range(n_groups):
        le = jnp.where(grp == g, lt[per_group * g:per_group * (g + 1), :], le)
    ee = jnp.exp(le - jnp.max(le, axis=0, keepdims=True))
    pe = ee / jnp.sum(ee, axis=0, keepdims=True)
    ei = lax.broadcasted_iota(I32, (per_group, tm), 0)
    p0 = jnp.max(pe, axis=0, keepdims=True)
    i0 = jnp.min(jnp.where(pe == p0, ei, per_group), axis=0, keepdims=True)
    pe2 = jnp.where(ei == i0, -1.0, pe)
    p1 = jnp.max(pe2, axis=0, keepdims=True)
    i1 = jnp.min(jnp.where(pe2 == p1, ei, per_group), axis=0, keepdims=True)
    psum = p0 + p1
    w0 = gval * (p0 / psum)
    w1 = gval * (p1 / psum)
    e0 = (grp * per_group + i0).astype(F32)
    e1 = (grp * per_group + i1).astype(F32)
    zero = jnp.zeros((SUBLANES - 4, tm), F32)
    route_ref[...] = jnp.concatenate([e0, e1, w0, w1, zero], axis=0)


def _mix(o, s5g, gates, x2, w_up_a, w_up_b, w_out, g_ffn, w_rg, b_rg, w_re, b_re):
    T, D = x2.shape
    tm = ROW_TILE
    n_groups = w_rg.shape[1]
    n_exp = w_re.shape[1]
    per_group = n_exp // n_groups
    n_r = -(-(n_exp + n_groups) // SUBLANES) * SUBLANES
    wr = jnp.zeros((n_r, D), BF16).at[:n_exp].set(w_re.T.astype(BF16)).at[n_exp:n_exp + n_groups].set(
        w_rg.T.astype(BF16))
    br = jnp.zeros((n_r, 1), F32).at[:n_exp, 0].set(b_re.astype(F32)).at[n_exp:n_exp + n_groups, 0].set(
        b_rg.astype(F32))
    wa, wb_, wo = w_up_a.astype(BF16), w_up_b.astype(BF16), w_out.astype(BF16)
    in_specs = [
        pl.BlockSpec((tm, o.shape[1]), lambda i: (i, 0)),
        pl.BlockSpec((tm, s5g.shape[1]), lambda i: (i, 0)),
        pl.BlockSpec((tm, gates.shape[1]), lambda i: (i, 0)),
        pl.BlockSpec((tm, D), lambda i: (i, 0)),
        _const_spec(wa.shape), _const_spec(wb_.shape), _const_spec(wo.shape), _const_spec((1, D)),
        _const_spec(wr.shape), _const_spec(br.shape),
    ]
    out_shape = (jax.ShapeDtypeStruct((T, D), F32), jax.ShapeDtypeStruct((T, D), BF16),
                 jax.ShapeDtypeStruct((SUBLANES, T), F32))
    out_specs = (pl.BlockSpec((tm, D), lambda i: (i, 0)), pl.BlockSpec((tm, D), lambda i: (i, 0)),
                 pl.BlockSpec((SUBLANES, tm), lambda i: (0, i)))
    return pl.pallas_call(
        functools.partial(_mix_kernel, n_groups=n_groups, per_group=per_group),
        grid=(T // tm,), in_specs=in_specs, out_specs=out_specs, out_shape=out_shape,
        compiler_params=_params(("parallel",)), name="mix_router",
    )(o, s5g, gates, x2, wa, wb_, wo, g_ffn.reshape(1, D), wr, br)


def _moe_kernel(be_ref, x_ref, rw_ref, wg_ref, wu_ref, wd_ref, y_ref):
    del be_ref
    xb = x_ref[...]
    g = _dot(xb, wg_ref[...])
    hb = (g * (1.0 / (1.0 + jnp.exp(-g)))) * _dot(xb, wu_ref[...])
    y_ref[...] = (_dot(hb.astype(BF16), wd_ref[...]) * rw_ref[...]).astype(BF16)


def _moe_ffn(xs, row_w, blk_exp, w_gate, w_up, w_down):
    n_rows, D = xs.shape
    E, _, De = w_gate.shape
    bm = MOE_ROWS
    grid_spec = pltpu.PrefetchScalarGridSpec(
        num_scalar_prefetch=1, grid=(n_rows // bm,),
        in_specs=[
            pl.BlockSpec((bm, D), lambda i, be: (i, 0)),
            pl.BlockSpec((bm, 1), lambda i, be: (i, 0)),
            pl.BlockSpec((None, D, De), lambda i, be: (be[i], 0, 0)),
            pl.BlockSpec((None, D, De), lambda i, be: (be[i], 0, 0)),
            pl.BlockSpec((None, De, D), lambda i, be: (be[i], 0, 0)),
        ],
        out_specs=pl.BlockSpec((bm, D), lambda i, be: (i, 0)),
    )
    return pl.pallas_call(
        _moe_kernel, grid_spec=grid_spec, out_shape=jax.ShapeDtypeStruct((n_rows, D), BF16),
        compiler_params=_params(("arbitrary",)), name="moe_ffn",
    )(blk_exp, xs, row_w, w_gate.astype(BF16), w_up.astype(BF16), w_down.astype(BF16))


def _final_kernel(x1_ref, y0_ref, y1_ref, g_ref, o_ref):
    o_ref[...] = _rms(x1_ref[...] + (y0_ref[...].astype(F32) + y1_ref[...].astype(F32)), g_ref[...])


def _final(x1, y0, y1, g_final):
    T, D = x1.shape
    tm = ROW_TILE
    spec = pl.BlockSpec((tm, D), lambda i: (i, 0))
    return pl.pallas_call(
        _final_kernel, grid=(T // tm,), in_specs=[spec, spec, spec, _const_spec((1, D))], out_specs=spec,
        out_shape=jax.ShapeDtypeStruct((T, D), F32), compiler_params=_params(("parallel",)), name="final_norm",
    )(x1, y0, y1, g_final.reshape(1, D))


def _dispatch_plan(eid, weight, n_experts, bm):
    T = eid.shape[0]
    M = 2 * T
    e_flat = eid.reshape(M)
    a_idx = jnp.arange(M, dtype=I32)
    _, order = lax.sort((e_flat, a_idx), num_keys=1)
    _, rank = lax.sort((order, a_idx), num_keys=1)
    experts = jnp.arange(n_experts, dtype=I32)
    counts = jnp.sum((e_flat[:, None] == experts[None, :]).astype(I32), axis=0)
    starts = jnp.cumsum(counts) - counts
    padded = (counts + bm - 1) // bm * bm
    pad_end = jnp.cumsum(padded)
    pad_start = pad_end - padded
    n_rows = (M + n_experts * (bm - 1) + bm - 1) // bm * bm
    blk_start = jnp.arange(n_rows // bm, dtype=I32) * bm
    blk_exp = jnp.minimum(jnp.sum((pad_end[None, :] <= blk_start[:, None]).astype(I32), axis=1), n_experts - 1)
    off = (blk_start - pad_start[blk_exp])[:, None] + jnp.arange(bm, dtype=I32)[None, :]
    valid = (off < counts[blk_exp][:, None]).reshape(n_rows)
    src = jnp.clip(starts[blk_exp][:, None] + off, 0, M - 1).reshape(n_rows)
    a_row = order[src]
    row_tok = jnp.where(valid, a_row // 2, 0)
    row_w = jnp.where(valid, weight.reshape(M)[a_row], 0.0)
    pos = pad_start[e_flat] + (rank - starts[e_flat])
    return row_tok, row_w, pos.reshape(T, 2), blk_exp


def kernel(x, g_mix, w_in, g_kv, w_uk, w_uv, rel_bias, s5_a_re, s5_a_im, s5_log_dt, s5_b_re, s5_b_im,
           s5_c_re, s5_c_im, s5_d, w_glu, w_up_a, w_up_b, b_gate, w_out, g_ffn, w_router_g, b_router_g,
           w_router_e, b_router_e, w_gate, w_up, w_down, g_final):
    B, S, D = x.shape
    T = B * S
    assert S % ROW_TILE == 0 and g_mix.shape[0] == 1
    x2 = x.reshape(T, D)
    for l in range(g_mix.shape[0]):
        qlat, c, qi, ki, wit, u, gates = _inproj(x2, g_mix[l], w_in[l], g_kv[l], w_uk[l], b_gate[l])
        o = _attention(qlat, c, qi, ki, wit, w_uv[l], rel_bias, B, S).reshape(T, -1)
        s5g = _s5(u.reshape(B, S, -1), s5_a_re[l], s5_a_im[l], s5_log_dt[l], s5_b_re[l], s5_b_im[l],
                  s5_c_re[l], s5_c_im[l], s5_d[l], w_glu[l]).reshape(T, -1)
        x1, h2, route = _mix(o, s5g, gates, x2, w_up_a[l], w_up_b[l], w_out[l], g_ffn[l],
                             w_router_g[l], b_router_g[l], w_router_e[l], b_router_e[l])
        eid = jnp.transpose(route[0:2]).astype(I32)
        weight = jnp.transpose(route[2:4])
        n_experts = w_gate.shape[1]
        row_tok, row_w, pos, blk_exp = _dispatch_plan(eid, weight, n_experts, MOE_ROWS)
        yb = _moe_ffn(h2[row_tok], row_w[:, None], blk_exp, w_gate[l], w_up[l], w_down[l])
        x2 = x1
        y0, y1 = yb[pos[:, 0]], yb[pos[:, 1]]
    out = _final(x2, y0, y1, g_final)
    return out.reshape(B, S, D)
```

```python
import functools
import math

import jax
import jax.numpy as jnp
from jax import lax
from jax.experimental import pallas as pl
from jax.experimental.pallas import tpu as pltpu

F32 = jnp.float32
BF16 = jnp.bfloat16
I32 = jnp.int32

CHUNK = 64
TOPK_MAX = 256
IDX_SCALE = 0.0625
N_BUCKETS = 32
MAX_DISTANCE = 128
EPS = 1e-6
SPLIT_POINTS = (512, 640, 896, 960, 964, 1476)

LANES = 128
SUBLANES = 8
VMEM_LIMIT_BYTES = 56 * 1024 * 1024

ROW_TILE = 512
ATTN_TQ = 128
ATTN_TK = 256
S5_TIME_CHUNK = 64
S5_COL_BLOCK = 512
MOE_ROWS = 512

INT_MIN = -(2 ** 31)
INT16_MIN = -(2 ** 15)
PACK = 16
LOG2E = math.log2(math.e)
ONES_ROWS = 16
NEG_MASK = -1e30
M_FLOOR = -1e20


def _dot(a, b):
    return jnp.dot(a, b, preferred_element_type=F32)


def _dot_nt(a, b):
    return lax.dot_general(a, b, (((1,), (1,)), ((), ())), preferred_element_type=F32)


def _rms(x, g):
    return x * lax.rsqrt(jnp.mean(x * x, axis=-1, keepdims=True) + EPS) * g


def _params(sem):
    return pltpu.CompilerParams(dimension_semantics=sem, vmem_limit_bytes=VMEM_LIMIT_BYTES)


def _const_spec(shape):
    n = len(shape)
    return pl.BlockSpec(shape, lambda *_: (0,) * n)


def _inproj_kernel(x_ref, g_ref, wq_ref, wuk_ref, wm_ref, gkv_ref, wwi_ref, wu_ref,
                   wgl_ref, bg_ref, qlat_ref, c_ref, qi_ref, ki_ref, wit_ref, u_ref, gate_ref, *, q_scale):
    hb = _rms(x_ref[...], g_ref[...]).astype(BF16)
    qb = _dot(hb, wq_ref[...]).astype(BF16)
    n_pair = wuk_ref.shape[0]
    for j in range(n_pair):
        ql = _dot(qb[:, LANES * j:LANES * (j + 1)], wuk_ref[j]) * q_scale
        qlat_ref[2 * j] = ql[:, :LANES].astype(BF16)
        qlat_ref[2 * j + 1] = ql[:, LANES:].astype(BF16)
    mid = _dot(hb, wm_ref[...])
    r = c_ref.shape[1]
    n_ih, _, idx_dim = qi_ref.shape
    c_ref[...] = _rms(mid[:, :r], gkv_ref[...]).astype(BF16)
    for h in range(n_ih):
        qi_ref[h] = mid[:, r + idx_dim * h:r + idx_dim * (h + 1)].astype(BF16)
    ki_ref[...] = mid[:, r + idx_dim * n_ih:r + idx_dim * (n_ih + 1)].astype(BF16)
    wit_ref[...] = _dot_nt(wwi_ref[...], hb) * IDX_SCALE
    u_ref[...] = _dot(hb, wu_ref[...])
    gl = _dot(hb, wgl_ref[...]) + bg_ref[...]
    gate_ref[...] = (1.0 / (1.0 + jnp.exp(-gl))).astype(BF16)


def _inproj(x2, g_mix, w_in, g_kv, w_uk, b_gate):
    T, D = x2.shape
    R, H, dh = w_uk.shape
    sp = SPLIT_POINTS
    n_ih = sp[4] - sp[3]
    idx_dim = (sp[2] - sp[1]) // n_ih
    tm = ROW_TILE
    wb = w_in.astype(BF16)
    wq = wb[:, :sp[0]]
    n_mid = sp[3] - sp[0]
    n_mid_pad = -(-n_mid // (2 * LANES)) * (2 * LANES)
    wm = jnp.concatenate([wb[:, sp[0]:sp[3]], jnp.zeros((D, n_mid_pad - n_mid), BF16)], axis=1)
    wwi = jnp.zeros((SUBLANES, D), BF16).at[:n_ih].set(wb[:, sp[3]:sp[4]].T)
    wu = wb[:, sp[4]:sp[5]]
    wgl = wb[:, sp[5]:]
    wuk_t = jnp.transpose(w_uk, (1, 2, 0)).astype(BF16)
    z = jnp.zeros((dh, R), BF16)
    wuk_pairs = jnp.stack([
        jnp.concatenate([jnp.concatenate([wuk_t[2 * j], z], axis=1),
                         jnp.concatenate([z, wuk_t[2 * j + 1]], axis=1)], axis=0)
        for j in range(H // 2)])
    n_gate = wgl.shape[1]
    n_u = wu.shape[1]
    out_shape = (
        jax.ShapeDtypeStruct((H, T, R), BF16),
        jax.ShapeDtypeStruct((T, R), BF16),
        jax.ShapeDtypeStruct((n_ih, T, idx_dim), BF16),
        jax.ShapeDtypeStruct((T, idx_dim), BF16),
        jax.ShapeDtypeStruct((SUBLANES, T), F32),
        jax.ShapeDtypeStruct((T, n_u), F32),
        jax.ShapeDtypeStruct((T, n_gate), BF16),
    )
    in_specs = [
        pl.BlockSpec((tm, D), lambda i: (i, 0)),
        _const_spec((1, D)), _const_spec(wq.shape), _const_spec(wuk_pairs.shape), _const_spec(wm.shape),
        _const_spec((1, R)), _const_spec(wwi.shape),
        _const_spec(wu.shape), _const_spec(wgl.shape), _const_spec((1, n_gate)),
    ]
    out_specs = (
        pl.BlockSpec((H, tm, R), lambda i: (0, i, 0)),
        pl.BlockSpec((tm, R), lambda i: (i, 0)),
        pl.BlockSpec((n_ih, tm, idx_dim), lambda i: (0, i, 0)),
        pl.BlockSpec((tm, idx_dim), lambda i: (i, 0)),
        pl.BlockSpec((SUBLANES, tm), lambda i: (0, i)),
        pl.BlockSpec((tm, n_u), lambda i: (i, 0)),
        pl.BlockSpec((tm, n_gate), lambda i: (i, 0)),
    )
    return pl.pallas_call(
        functools.partial(_inproj_kernel, q_scale=dh ** -0.5 * LOG2E),
        grid=(T // tm,), in_specs=in_specs, out_specs=out_specs, out_shape=out_shape,
        compiler_params=_params(("parallel",)), name="inproj",
    )(x2, g_mix.reshape(1, D), wq, wuk_pairs, wm, g_kv.reshape(1, R), wwi, wu, wgl,
      b_gate.reshape(1, n_gate))


def _attn_kernel(qlat_ref, qi_ref, wit_ref, ki_ref, c_ref, ct_ref, bias_ref, tri_ref, eye_ref, wuv_ref, o_ref,
                 keys_ref, hi_ref, lo_ref, mask_ref, qa_ref, lga_ref, lgb_ref, tma_ref, tmb_ref, m_ref, acc_ref,
                 tie_ref, *, top_k):
    H, TQ, R = qlat_ref.shape
    n_ih = qi_ref.shape[0]
    TK = tri_ref.shape[0]
    j = pl.program_id(1)
    q0 = j * TQ
    n_kt = (q0 + TQ + TK - 1) // TK
    n_pair = (n_kt + 1) // 2
    n_quad = n_pair // 2
    HT = TK // 2

    lane = lax.broadcasted_iota(I32, (1, TQ), 1)
    limit = ((q0 + lane) // CHUNK + 1) * CHUNK

    qi_all = qi_ref[...].reshape(n_ih * TQ, qi_ref.shape[2])
    wt = wit_ref[...]

    def score_tile(kt):
        k0 = pl.multiple_of(kt * TK, TK)
        sc = _dot_nt(ki_ref[pl.ds(k0, TK), :], qi_all)
        s = jnp.zeros((TK, TQ), F32)
        for h in range(n_ih):
            s = s + wt[h:h + 1, :] * jnp.maximum(sc[:, h * TQ:(h + 1) * TQ], 0.0)
        s = jnp.where(s == 0.0, 0.0, s)
        bits = lax.bitcast_convert_type(s, I32)
        key = bits ^ ((bits >> 31) & 0x7FFFFFFF)
        row = k0 + lax.broadcasted_iota(I32, (TK, 1), 0)
        key = jnp.where(row < limit, key, INT_MIN)
        keys_ref[pl.ds(k0, TK), :] = key
        hi_ref[pl.ds(pl.multiple_of(kt * HT, HT), HT), :] = (key[:HT] & -65536) | ((key[HT:] >> 16) & 0xFFFF)

    def score_quad(i, carry):
        for t in range(4):
            score_tile(4 * i + t)
        return carry

    def score_rest(i, carry):
        score_tile(2 * i)
        score_tile(2 * i + 1)
        return carry

    lax.fori_loop(0, n_quad, score_quad, 0)
    lax.fori_loop(2 * n_quad, n_pair, score_rest, 0)

    one16, zero16 = jnp.int16(1), jnp.int16(0)

    def count16(ref, pred):
        def step(pair, n_pairs, acc):
            words = ref[pl.ds(pl.multiple_of(pair * TK, TK), n_pairs * TK), :]
            ind = jnp.where(pred(pltpu.bitcast(words, jnp.int16)), one16, zero16)
            parts = [ind[PACK * r:PACK * (r + 1)] for r in range(n_pairs * 2 * TK // PACK)]
            while len(parts) > 1:
                parts = [a + b for a, b in zip(parts[::2], parts[1::2])]
            return acc + parts[0]
        acc = lax.fori_loop(0, n_quad, lambda i, a: step(2 * i, 2, a), jnp.zeros((PACK, TQ), jnp.int16))
        acc = lax.fori_loop(2 * n_quad, n_pair, lambda i, a: step(i, 1, a), acc).astype(I32)
        return (acc[:SUBLANES] + acc[SUBLANES:]).sum(axis=0, keepdims=True)

    def kth_largest16(ref, k):
        def bit_body(i, lo):
            cand = lo + lax.shift_left(jnp.int32(1), 15 - i)
            c16 = cand.astype(jnp.int16)
            cnt = count16(ref, lambda blk: blk >= c16)
            return jnp.where(cnt >= k, cand, lo)
        return lax.fori_loop(0, 16, bit_body, jnp.full((1, TQ), INT16_MIN, I32))

    t_hi = kth_largest16(hi_ref, top_k)
    t_hi16 = t_hi.astype(jnp.int16)
    n_hi_gt = count16(hi_ref, lambda blk: blk > t_hi16)

    def low_tile(kt, carry):
        key = keys_ref[pl.ds(pl.multiple_of(kt * TK, TK), TK), :]
        low = jnp.where((key >> 16) == t_hi, (key ^ 0x8000) & 0xFFFF, 0x8000)
        lo_ref[pl.ds(pl.multiple_of(kt * HT, HT), HT), :] = (low[:HT] << 16) | low[HT:]
        return carry

    lax.fori_loop(0, 2 * n_pair, low_tile, 0)
    t_lo = kth_largest16(lo_ref, top_k - n_hi_gt)
    t_lo16 = t_lo.astype(jnp.int16)
    thr = (t_hi << 16) | ((t_lo ^ 0x8000) & 0xFFFF)
    n_gt = n_hi_gt + count16(lo_ref, lambda blk: blk > t_lo16)
    n_tie = jnp.where(thr == INT_MIN, 0, top_k - n_gt).astype(F32)

    tie_ref[...] = jnp.zeros(tie_ref.shape, F32)

    def mask_tiles(first, n_tiles):
        ks = [pl.multiple_of((first + t) * TK, TK) for t in range(n_tiles)]
        keys = [keys_ref[pl.ds(k0, TK), :] for k0 in ks]
        eqs = [key == thr for key in keys]
        ranks = [_dot(tri_ref[...], jnp.where(eq, 1.0, 0.0).astype(BF16)) for eq in eqs]
        tie = tie_ref[...]
        for k0, key, eq, rank in zip(ks, keys, eqs, ranks):
            rank = rank + tie
            tie = rank[TK - 1:TK, :]
            negmask = jnp.where(key > thr, 0.0,
                                jnp.where(eq, jnp.where(rank <= n_tie, 0.0, NEG_MASK), NEG_MASK))
            mask_ref[pl.ds(k0, TK), :] = negmask.astype(BF16)
        tie_ref[...] = tie

    def mask_quad(i, carry):
        mask_tiles(4 * i, 4)
        return carry

    def mask_rest(i, carry):
        mask_tiles(2 * i, 2)
        return carry

    lax.fori_loop(0, n_quad, mask_quad, 0)
    lax.fori_loop(2 * n_quad, n_pair, mask_rest, 0)

    qa_ref[...] = jnp.concatenate([qlat_ref[...].reshape(H * TQ, R), eye_ref[...]], axis=1)
    m_ref[...] = jnp.full(m_ref.shape, M_FLOOR, F32)
    acc_ref[...] = jnp.zeros(acc_ref.shape, F32)

    def issue_logits(kt, lg_ref, tm_ref):
        k0 = pl.multiple_of(jnp.minimum(kt, 2 * n_pair - 1) * TK, TK)
        c_aug = jnp.concatenate([c_ref[pl.ds(k0, TK), :], mask_ref[pl.ds(k0, TK), :]], axis=1)
        logits = _dot_nt(c_aug, qa_ref[...])
        boff = pl.multiple_of(jnp.clip(k0 - q0 + 2 * TK, 0, 2 * TK), LANES)
        for h in range(H):
            sl = slice(h * TQ, (h + 1) * TQ)
            lg = logits[:, sl] + bias_ref[h, pl.ds(boff, TK), :]
            lg_ref[:, sl] = lg
            tm_ref[:, sl] = jnp.max(lg, axis=0, keepdims=True)

    def softmax_pv(kt, lg_ref, tm_ref):
        ct_tile = ct_ref[kt]
        for hp in range(H // 2):
            cols = slice(2 * hp * TQ, 2 * (hp + 1) * TQ)
            m_old = m_ref[:, cols]
            m_new = jnp.maximum(m_old, tm_ref[:, cols])
            m_ref[:, cols] = m_new
            alpha = jnp.exp2(m_old - m_new)
            p = jnp.exp2(lg_ref[:, cols] - m_new).astype(BF16)
            acc_ref[:, cols] = acc_ref[:, cols] * alpha + _dot(ct_tile, p)

    issue_logits(0, lga_ref, tma_ref)

    def flash_pair(i):
        issue_logits(2 * i + 1, lgb_ref, tmb_ref)
        softmax_pv(2 * i, lga_ref, tma_ref)
        issue_logits(2 * i + 2, lga_ref, tma_ref)
        softmax_pv(2 * i + 1, lgb_ref, tmb_ref)

    def flash_quad(i, carry):
        flash_pair(2 * i)
        flash_pair(2 * i + 1)
        return carry

    def flash_rest(i, carry):
        flash_pair(i)
        return carry

    lax.fori_loop(0, n_quad, flash_quad, 0)
    lax.fori_loop(2 * n_quad, n_pair, flash_rest, 0)

    olat = acc_ref[:R, :] * (1.0 / acc_ref[R:R + 1, :])
    o = jnp.zeros((TQ, wuv_ref.shape[2]), F32)
    for h in range(H):
        o = o + _dot(olat[:, h * TQ:(h + 1) * TQ].T.astype(BF16), wuv_ref[h])
    o_ref[...] = o.astype(BF16)


def _t5_bucket(rel):
    half = N_BUCKETS // 2
    max_exact = half // 2
    ret = jnp.where(rel > 0, half, 0)
    n = jnp.abs(rel)
    nf = jnp.maximum(n, 1).astype(F32)
    large = max_exact + (jnp.log(nf / max_exact) / math.log(MAX_DISTANCE / max_exact)
                         * (half - max_exact)).astype(I32)
    large = jnp.minimum(large, half - 1)
    return ret + jnp.where(n < max_exact, n, large)


def _attention(qlat, c, qi, ki, wit, w_uv, rel_bias, B, S):
    H, T, R = qlat.shape
    n_ih, _, idx_dim = qi.shape
    dh = w_uv.shape[2]
    TQ, TK = ATTN_TQ, ATTN_TK
    assert S % (2 * TK) == 0 and S % TQ == 0 and TK % TQ == 0 and MAX_DISTANCE <= TQ
    top_k = min(TOPK_MAX, S // 4)
    n_qt = S // TQ
    i = jnp.arange(3 * TK, dtype=I32)[:, None] - 2 * TK
    jq = jnp.arange(TQ, dtype=I32)[None, :]
    bucket = _t5_bucket(i - jq)[None]
    rb = (rel_bias - rel_bias[_t5_bucket(jnp.int32(-MAX_DISTANCE))][None, :]).astype(F32).T
    bias_t = jnp.zeros((H,) + bucket.shape[1:], F32)
    for k in range(N_BUCKETS):
        bias_t = jnp.where(bucket == k, rb[:, k][:, None, None], bias_t)
    bias_t = bias_t * LOG2E
    tri = (jnp.arange(TK)[:, None] >= jnp.arange(TK)[None, :]).astype(BF16)
    eye = jnp.tile(jnp.eye(TQ, dtype=BF16), (H, 1))
    wuv_h = jnp.transpose(w_uv, (1, 0, 2)).astype(BF16)
    wuv_pad = jnp.zeros((H, R, H * dh), BF16)
    for h in range(H):
        wuv_pad = wuv_pad.at[h, :, h * dh:(h + 1) * dh].set(wuv_h[h])
    c3 = c.reshape(B, S, R)
    ct = jnp.transpose(c3.reshape(B, S // TK, TK, R), (0, 1, 3, 2))
    ones_rows = jnp.zeros((B, S // TK, ONES_ROWS, TK), BF16).at[:, :, 0, :].set(1.0)
    ct = jnp.concatenate([ct, ones_rows], axis=2)
    grid = (B, n_qt)
    in_specs = [
        pl.BlockSpec((H, None, TQ, R), lambda b, j: (0, b, j, 0)),
        pl.BlockSpec((n_ih, None, TQ, idx_dim), lambda b, j: (0, b, j, 0)),
        pl.BlockSpec((SUBLANES, TQ), lambda b, j: (0, b * n_qt + j)),
        pl.BlockSpec((None, S, idx_dim), lambda b, j: (b, 0, 0)),
        pl.BlockSpec((None, S, R), lambda b, j: (b, 0, 0)),
        pl.BlockSpec((None, S // TK, R + ONES_ROWS, TK), lambda b, j: (b, 0, 0, 0)),
        _const_spec(bias_t.shape), _const_spec(tri.shape), _const_spec(eye.shape), _const_spec(wuv_pad.shape),
    ]
    return pl.pallas_call(
        functools.partial(_attn_kernel, top_k=top_k),
        grid=grid, in_specs=in_specs,
        out_specs=pl.BlockSpec((None, TQ, H * dh), lambda b, j: (b, j, 0)),
        out_shape=jax.ShapeDtypeStruct((B, S, H * dh), BF16),
        scratch_shapes=[pltpu.VMEM((S, TQ), I32), pltpu.VMEM((S // 2, TQ), I32), pltpu.VMEM((S // 2, TQ), I32),
                        pltpu.VMEM((S, TQ), BF16),
                        pltpu.VMEM((H * TQ, R + TQ), BF16), pltpu.VMEM((TK, H * TQ), F32),
                        pltpu.VMEM((TK, H * TQ), F32), pltpu.VMEM((1, H * TQ), F32), pltpu.VMEM((1, H * TQ), F32),
                        pltpu.VMEM((1, H * TQ), F32),
                        pltpu.VMEM((R + ONES_ROWS, H * TQ), F32), pltpu.VMEM((1, TQ), F32)],
        compiler_params=_params(("parallel", "arbitrary")), name="dsa_attention",
    )(qlat.reshape(H, B, S, R), qi.reshape(n_ih, B, S, idx_dim), wit, ki.reshape(B, S, idx_dim), c3, ct,
      bias_t, tri, eye, wuv_pad)


def _s5_kernel(u_ref, ar_ref, ai_ref, bbr_ref, bbi_ref, ccr_ref, cci_ref, d_ref, wglu_ref, o_ref,
               xr_ref, xi_ref, sr_ref, si_ref):
    batch, steps, width = u_ref.shape
    rows = batch * steps
    n_state = ar_ref.shape[1]
    n_half = bbr_ref.shape[0]
    uw = width // n_half
    sw = n_state // n_half

    @pl.when(pl.program_id(0) == 0)
    def _():
        sr_ref[...] = jnp.zeros(sr_ref.shape, F32)
        si_ref[...] = jnp.zeros(si_ref.shape, F32)

    u = jnp.swapaxes(u_ref[...], 0, 1).reshape(rows, width)
    ub = u.astype(BF16)
    for j in range(n_half):
        uj = ub[:, uw * j:uw * (j + 1)]
        xr_ref[:, sw * j:sw * (j + 1)] = _dot(uj, bbr_ref[j])
        xi_ref[:, sw * j:sw * (j + 1)] = _dot(uj, bbi_ref[j])

    cb = S5_COL_BLOCK
    for k in range(n_state // cb):
        cs = slice(cb * k, cb * (k + 1))
        ar = jnp.broadcast_to(ar_ref[:, cs], (batch, cb))
        ai = jnp.broadcast_to(ai_ref[:, cs], (batch, cb))

        def step(t, carry):
            xr, xi = carry
            r0 = pl.multiple_of(t * batch, batch)
            nr = ar * xr - ai * xi + xr_ref[pl.ds(r0, batch), cs]
            ni = ar * xi + ai * xr + xi_ref[pl.ds(r0, batch), cs]
            xr_ref[pl.ds(r0, batch), cs] = nr
            xi_ref[pl.ds(r0, batch), cs] = ni
            return nr, ni

        xr, xi = lax.fori_loop(0, steps, step, (sr_ref[:, cs], si_ref[:, cs]), unroll=8)
        sr_ref[:, cs] = xr
        si_ref[:, cs] = xi

    ys = []
    for j in range(n_half):
        xs_r = xr_ref[:, sw * j:sw * (j + 1)].astype(BF16)
        xs_i = xi_ref[:, sw * j:sw * (j + 1)].astype(BF16)
        ys.append(_dot(xs_r, ccr_ref[j]) - _dot(xs_i, cci_ref[j]))
    y = jnp.concatenate(ys, axis=1) + d_ref[...] * u
    z = 0.5 * y * (1.0 + jnp.tanh(math.sqrt(2.0 / math.pi) * (y + 0.044715 * (y * y * y))))
    vg = _dot(z.astype(BF16), wglu_ref[...])
    half = vg.shape[1] // 2
    out = vg[:, :half] * (1.0 / (1.0 + jnp.exp(-vg[:, half:])))
    o_ref[...] = jnp.swapaxes(out.reshape(steps, batch, half), 0, 1).astype(BF16)


def _block_diag_halves(blocks, n_half):
    G, r, c = blocks.shape
    g2 = G // n_half
    eye = jnp.eye(g2, dtype=blocks.dtype)
    b = blocks.reshape(n_half, g2, r, c)
    return jnp.einsum('jgrc,gh->jgrhc', b, eye).reshape(n_half, g2 * r, g2 * c)


def _s5(u3, a_re, a_im, log_dt, b_re, b_im, c_re, c_im, d_skip, w_glu):
    batch, seq, width = u3.shape
    G, N = a_re.shape
    P = b_re.shape[2]
    n_state = G * N
    n_half = 2
    dt = jnp.exp(log_dt.astype(F32))[:, None]
    ar, ai = a_re.astype(F32), a_im.astype(F32)
    mag = jnp.exp(ar * dt)
    abar_r = mag * jnp.cos(ai * dt)
    abar_i = mag * jnp.sin(ai * dt)
    nr, ni = abar_r - 1.0, abar_i
    den = ar * ar + ai * ai
    fr = (nr * ar + ni * ai) / den
    fi = (ni * ar - nr * ai) / den
    br_, bi_ = b_re.astype(F32), b_im.astype(F32)
    bb_r = fr[..., None] * br_ - fi[..., None] * bi_
    bb_i = fr[..., None] * bi_ + fi[..., None] * br_
    bbr = _block_diag_halves(jnp.transpose(bb_r, (0, 2, 1)), n_half).astype(BF16)
    bbi = _block_diag_halves(jnp.transpose(bb_i, (0, 2, 1)), n_half).astype(BF16)
    ccr = _block_diag_halves(jnp.transpose(c_re.astype(F32), (0, 2, 1)), n_half).astype(BF16)
    cci = _block_diag_halves(jnp.transpose(c_im.astype(F32), (0, 2, 1)), n_half).astype(BF16)
    tc = S5_TIME_CHUNK
    rows = tc * batch
    n_out = w_glu.shape[1] // 2
    assert batch % SUBLANES == 0 and seq % tc == 0 and n_state % S5_COL_BLOCK == 0
    in_specs = [
        pl.BlockSpec((batch, tc, width), lambda i: (0, i, 0)),
        _const_spec((1, n_state)), _const_spec((1, n_state)),
        _const_spec(bbr.shape), _const_spec(bbi.shape), _const_spec(ccr.shape), _const_spec(cci.shape),
        _const_spec((1, width)), _const_spec(w_glu.shape),
    ]
    return pl.pallas_call(
        _s5_kernel, grid=(seq // tc,), in_specs=in_specs,
        out_specs=pl.BlockSpec((batch, tc, n_out), lambda i: (0, i, 0)),
        out_shape=jax.ShapeDtypeStruct((batch, seq, n_out), BF16),
        scratch_shapes=[pltpu.VMEM((rows, n_state), F32), pltpu.VMEM((rows, n_state), F32),
                        pltpu.VMEM((batch, n_state), F32), pltpu.VMEM((batch, n_state), F32)],
        compiler_params=_params(("arbitrary",)), name="s5_mixer",
    )(u3, abar_r.reshape(1, n_state), abar_i.reshape(1, n_state), bbr, bbi, ccr, cci,
      d_skip.astype(F32).reshape(1, width), w_glu.astype(BF16))


def _mix_kernel(o_ref, s_ref, gate_ref, x_ref, wa_ref, wb_ref, wo_ref, gf_ref, wr_ref, br_ref,
                x1_ref, h2_ref, route_ref, *, n_groups, per_group):
    d = x_ref.shape[1]
    tm = x_ref.shape[0]
    ya = _dot(o_ref[...], wa_ref[...])
    yb = _dot(s_ref[...], wb_ref[...])
    mix = gate_ref[:, :d].astype(F32) * ya + gate_ref[:, d:].astype(F32) * yb
    x1 = x_ref[...] + _dot(mix.astype(BF16), wo_ref[...])
    x1_ref[...] = x1
    h2b = _rms(x1, gf_ref[...]).astype(BF16)
    h2_ref[...] = h2b
    lt = _dot_nt(wr_ref[...], h2b) + br_ref[...]
    n_exp = n_groups * per_group
    lg = lt[n_exp:n_exp + n_groups, :]
    eg = jnp.exp(lg - jnp.max(lg, axis=0, keepdims=True))
    pg = eg / jnp.sum(eg, axis=0, keepdims=True)
    gval = jnp.max(pg, axis=0, keepdims=True)
    gi = lax.broadcasted_iota(I32, (n_groups, tm), 0)
    grp = jnp.min(jnp.where(pg == gval, gi, n_groups), axis=0, keepdims=True)
    le = jnp.zeros((per_group, tm), F32)
    for g in range(n_groups):
        le = jnp.where(grp == g, lt[per_group * g:per_group * (g + 1), :], le)
    ee = jnp.exp(le - jnp.max(le, axis=0, keepdims=True))
    pe = ee / jnp.sum(ee, axis=0, keepdims=True)
    ei = lax.broadcasted_iota(I32, (per_group, tm), 0)
    p0 = jnp.max(pe, axis=0, keepdims=True)
    i0 = jnp.min(jnp.where(pe == p0, ei, per_group), axis=0, keepdims=True)
    pe2 = jnp.where(ei == i0, -1.0, pe)
    p1 = jnp.max(pe2, axis=0, keepdims=True)
    i1 = jnp.min(jnp.where(pe2 == p1, ei, per_group), axis=0, keepdims=True)
    psum = p0 + p1
    w0 = gval * (p0 / psum)
    w1 = gval * (p1 / psum)
    e0 = (grp * per_group + i0).astype(F32)
    e1 = (grp * per_group + i1).astype(F32)
    zero = jnp.zeros((SUBLANES - 4, tm), F32)
    route_ref[...] = jnp.concatenate([e0, e1, w0, w1, zero], axis=0)


def _mix(o, s5g, gates, x2, w_up_a, w_up_b, w_out, g_ffn, w_rg, b_rg, w_re, b_re):
    T, D = x2.shape
    tm = ROW_TILE
    n_groups = w_rg.shape[1]
    n_exp = w_re.shape[1]
    per_group = n_exp // n_groups
    n_r = -(-(n_exp + n_groups) // SUBLANES) * SUBLANES
    wr = jnp.zeros((n_r, D), BF16).at[:n_exp].set(w_re.T.astype(BF16)).at[n_exp:n_exp + n_groups].set(
        w_rg.T.astype(BF16))
    br = jnp.zeros((n_r, 1), F32).at[:n_exp, 0].set(b_re.astype(F32)).at[n_exp:n_exp + n_groups, 0].set(
        b_rg.astype(F32))
    wa, wb_, wo = w_up_a.astype(BF16), w_up_b.astype(BF16), w_out.astype(BF16)
    in_specs = [
        pl.BlockSpec((tm, o.shape[1]), lambda i: (i, 0)),
        pl.BlockSpec((tm, s5g.shape[1]), lambda i: (i, 0)),
        pl.BlockSpec((tm, gates.shape[1]), lambda i: (i, 0)),
        pl.BlockSpec((tm, D), lambda i: (i, 0)),
        _const_spec(wa.shape), _const_spec(wb_.shape), _const_spec(wo.shape), _const_spec((1, D)),
        _const_spec(wr.shape), _const_spec(br.shape),
    ]
    out_shape = (jax.ShapeDtypeStruct((T, D), F32), jax.ShapeDtypeStruct((T, D), BF16),
                 jax.ShapeDtypeStruct((SUBLANES, T), F32))
    out_specs = (pl.BlockSpec((tm, D), lambda i: (i, 0)), pl.BlockSpec((tm, D), lambda i: (i, 0)),
                 pl.BlockSpec((SUBLANES, tm), lambda i: (0, i)))
    return pl.pallas_call(
        functools.partial(_mix_kernel, n_groups=n_groups, per_group=per_group),
        grid=(T // tm,), in_specs=in_specs, out_specs=out_specs, out_shape=out_shape,
        compiler_params=_params(("parallel",)), name="mix_router",
    )(o, s5g, gates, x2, wa, wb_, wo, g_ffn.reshape(1, D), wr, br)


def _moe_kernel(be_ref, x_ref, rw_ref, wg_ref, wu_ref, wd_ref, y_ref):
    del be_ref
    xb = x_ref[...]
    g = _dot(xb, wg_ref[...])
    hb = (g * (1.0 / (1.0 + jnp.exp(-g)))) * _dot(xb, wu_ref[...])
    rw = jnp.transpose(rw_ref[...])[:, 0:1]
    y_ref[...] = (_dot(hb.astype(BF16), wd_ref[...]) * rw).astype(BF16)


def _moe_ffn(xs, row_w, blk_exp, w_gate, w_up, w_down):
    n_rows, D = xs.shape
    E, _, De = w_gate.shape
    bm = MOE_ROWS
    rw3 = jnp.pad(row_w.reshape(n_rows // bm, 1, bm), ((0, 0), (0, SUBLANES - 1), (0, 0)))
    grid_spec = pltpu.PrefetchScalarGridSpec(
        num_scalar_prefetch=1, grid=(n_rows // bm,),
        in_specs=[
            pl.BlockSpec((bm, D), lambda i, be: (i, 0)),
            pl.BlockSpec((None, SUBLANES, bm), lambda i, be: (i, 0, 0)),
            pl.BlockSpec((None, D, De), lambda i, be: (be[i], 0, 0)),
            pl.BlockSpec((None, D, De), lambda i, be: (be[i], 0, 0)),
            pl.BlockSpec((None, De, D), lambda i, be: (be[i], 0, 0)),
        ],
        out_specs=pl.BlockSpec((bm, D), lambda i, be: (i, 0)),
    )
    return pl.pallas_call(
        _moe_kernel, grid_spec=grid_spec, out_shape=jax.ShapeDtypeStruct((n_rows, D), BF16),
        compiler_params=_params(("arbitrary",)), name="moe_ffn",
    )(blk_exp, xs, rw3, w_gate.astype(BF16), w_up.astype(BF16), w_down.astype(BF16))


def _final_kernel(x1_ref, y0_ref, y1_ref, g_ref, o_ref):
    o_ref[...] = _rms(x1_ref[...] + (y0_ref[...].astype(F32) + y1_ref[...].astype(F32)), g_ref[...])


def _final(x1, y0, y1, g_final):
    T, D = x1.shape
    tm = ROW_TILE
    spec = pl.BlockSpec((tm, D), lambda i: (i, 0))
    return pl.pallas_call(
        _final_kernel, grid=(T // tm,), in_specs=[spec, spec, spec, _const_spec((1, D))], out_specs=spec,
        out_shape=jax.ShapeDtypeStruct((T, D), F32), compiler_params=_params(("parallel",)), name="final_norm",
    )(x1, y0, y1, g_final.reshape(1, D))


def _dispatch_plan(eid, weight, n_experts, bm):
    T = eid.shape[0]
    M = 2 * T
    e_flat = eid.reshape(M)
    a_idx = jnp.arange(M, dtype=I32)
    _, order = lax.sort((e_flat, a_idx), num_keys=1)
    _, rank = lax.sort((order, a_idx), num_keys=1)
    experts = jnp.arange(n_experts, dtype=I32)
    counts = jnp.sum((e_flat[:, None] == experts[None, :]).astype(I32), axis=0)
    starts = jnp.cumsum(counts) - counts
    padded = (counts + bm - 1) // bm * bm
    pad_end = jnp.cumsum(padded)
    pad_start = pad_end - padded
    n_rows = (M + n_experts * (bm - 1) + bm - 1) // bm * bm
    blk_start = jnp.arange(n_rows // bm, dtype=I32) * bm
    blk_exp = jnp.minimum(jnp.sum((pad_end[None, :] <= blk_start[:, None]).astype(I32), axis=1), n_experts - 1)
    off = (blk_start - pad_start[blk_exp])[:, None] + jnp.arange(bm, dtype=I32)[None, :]
    valid = (off < counts[blk_exp][:, None]).reshape(n_rows)
    src = jnp.clip(starts[blk_exp][:, None] + off, 0, M - 1).reshape(n_rows)
    a_row = order[src]
    row_tok = jnp.where(valid, a_row // 2, 0)
    row_w = jnp.where(valid, weight.reshape(M)[a_row], 0.0)
    pos = pad_start[e_flat] + (rank - starts[e_flat])
    return row_tok, row_w, pos.reshape(T, 2), blk_exp


def kernel(x, g_mix, w_in, g_kv, w_uk, w_uv, rel_bias, s5_a_re, s5_a_im, s5_log_dt, s5_b_re, s5_b_im,
           s5_c_re, s5_c_im, s5_d, w_glu, w_up_a, w_up_b, b_gate, w_out, g_ffn, w_router_g, b_router_g,
           w_router_e, b_router_e, w_gate, w_up, w_down, g_final):
    B, S, D = x.shape
    T = B * S
    assert S % ROW_TILE == 0 and g_mix.shape[0] == 1
    x2 = x.reshape(T, D)
    for l in range(g_mix.shape[0]):
        qlat, c, qi, ki, wit, u, gates = _inproj(x2, g_mix[l], w_in[l], g_kv[l], w_uk[l], b_gate[l])
        o = _attention(qlat, c, qi, ki, wit, w_uv[l], rel_bias, B, S).reshape(T, -1)
        s5g = _s5(u.reshape(B, S, -1), s5_a_re[l], s5_a_im[l], s5_log_dt[l], s5_b_re[l], s5_b_im[l],
                  s5_c_re[l], s5_c_im[l], s5_d[l], w_glu[l]).reshape(T, -1)
        x1, h2, route = _mix(o, s5g, gates, x2, w_up_a[l], w_up_b[l], w_out[l], g_ffn[l],
                             w_router_g[l], b_router_g[l], w_router_e[l], b_router_e[l])
        eid = jnp.transpose(route[0:2]).astype(I32)
        weight = jnp.transpose(route[2:4])
        n_experts = w_gate.shape[1]
        row_tok, row_w, pos, blk_exp = _dispatch_plan(eid, weight, n_experts, MOE_ROWS)
        yb = _moe_ffn(h2[row_tok], row_w, blk_exp, w_gate[l], w_up[l], w_down[l])
        x2 = x1
        y0, y1 = yb[pos[:, 0]], yb[pos[:, 1]]
    out = _final(x2, y0, y1, g_final)
    return out.reshape(B, S, D)
```

```python
import functools
import math

import jax
import jax.numpy as jnp
from jax import lax
from jax.experimental import pallas as pl
from jax.experimental.pallas import tpu as pltpu

F32 = jnp.float32
BF16 = jnp.bfloat16
I32 = jnp.int32

CHUNK = 64
TOPK_MAX = 256
IDX_SCALE = 0.0625
N_BUCKETS = 32
MAX_DISTANCE = 128
EPS = 1e-6
SPLIT_POINTS = (512, 640, 896, 960, 964, 1476)

LANES = 128
SUBLANES = 8
VMEM_LIMIT_BYTES = 56 * 1024 * 1024

ROW_TILE = 512
ATTN_TQ = 128
ATTN_TK = 256
S5_TIME_CHUNK = 64
S5_COL_BLOCK = 512
MOE_ROWS = 512

INT_MIN = -(2 ** 31)
INT16_MIN = -(2 ** 15)
PACK = 16
LOG2E = math.log2(math.e)
ONES_ROWS = 16
NEG_MASK = -1e30
M_FLOOR = -1e20


def _dot(a, b):
    return jnp.dot(a, b, preferred_element_type=F32)


def _dot_nt(a, b):
    return lax.dot_general(a, b, (((1,), (1,)), ((), ())), preferred_element_type=F32)


def _rms(x, g):
    return x * lax.rsqrt(jnp.mean(x * x, axis=-1, keepdims=True) + EPS) * g


def _params(sem):
    return pltpu.CompilerParams(dimension_semantics=sem, vmem_limit_bytes=VMEM_LIMIT_BYTES)


def _const_spec(shape):
    n = len(shape)
    return pl.BlockSpec(shape, lambda *_: (0,) * n)


def _inproj_kernel(x_ref, g_ref, wq_ref, wuk_ref, wm_ref, gkv_ref, wwi_ref, wu_ref,
                   wgl_ref, bg_ref, qlat_ref, c_ref, qi_ref, ki_ref, wit_ref, u_ref, gate_ref, *, q_scale):
    hb = _rms(x_ref[...], g_ref[...]).astype(BF16)
    qb = _dot(hb, wq_ref[...]).astype(BF16)
    n_pair = wuk_ref.shape[0]
    for j in range(n_pair):
        ql = _dot(qb[:, LANES * j:LANES * (j + 1)], wuk_ref[j]) * q_scale
        qlat_ref[2 * j] = ql[:, :LANES].astype(BF16)
        qlat_ref[2 * j + 1] = ql[:, LANES:].astype(BF16)
    mid = _dot(hb, wm_ref[...])
    r = c_ref.shape[1]
    n_ih, _, idx_dim = qi_ref.shape
    c_ref[...] = _rms(mid[:, :r], gkv_ref[...]).astype(BF16)
    for h in range(n_ih):
        qi_ref[h] = mid[:, r + idx_dim * h:r + idx_dim * (h + 1)].astype(BF16)
    ki_ref[...] = mid[:, r + idx_dim * n_ih:r + idx_dim * (n_ih + 1)].astype(BF16)
    wit_ref[...] = _dot_nt(wwi_ref[...], hb) * IDX_SCALE
    u_ref[...] = _dot(hb, wu_ref[...])
    gl = _dot(hb, wgl_ref[...]) + bg_ref[...]
    gate_ref[...] = (1.0 / (1.0 + jnp.exp(-gl))).astype(BF16)


def _inproj(x2, g_mix, w_in, g_kv, w_uk, b_gate):
    T, D = x2.shape
    R, H, dh = w_uk.shape
    sp = SPLIT_POINTS
    n_ih = sp[4] - sp[3]
    idx_dim = (sp[2] - sp[1]) // n_ih
    tm = ROW_TILE
    wb = w_in.astype(BF16)
    wq = wb[:, :sp[0]]
    n_mid = sp[3] - sp[0]
    n_mid_pad = -(-n_mid // (2 * LANES)) * (2 * LANES)
    wm = jnp.concatenate([wb[:, sp[0]:sp[3]], jnp.zeros((D, n_mid_pad - n_mid), BF16)], axis=1)
    wwi = jnp.zeros((SUBLANES, D), BF16).at[:n_ih].set(wb[:, sp[3]:sp[4]].T)
    wu = wb[:, sp[4]:sp[5]]
    wgl = wb[:, sp[5]:]
    wuk_t = jnp.transpose(w_uk, (1, 2, 0)).astype(BF16)
    z = jnp.zeros((dh, R), BF16)
    wuk_pairs = jnp.stack([
        jnp.concatenate([jnp.concatenate([wuk_t[2 * j], z], axis=1),
                         jnp.concatenate([z, wuk_t[2 * j + 1]], axis=1)], axis=0)
        for j in range(H // 2)])
    n_gate = wgl.shape[1]
    n_u = wu.shape[1]
    out_shape = (
        jax.ShapeDtypeStruct((H, T, R), BF16),
        jax.ShapeDtypeStruct((T, R), BF16),
        jax.ShapeDtypeStruct((n_ih, T, idx_dim), BF16),
        jax.ShapeDtypeStruct((T, idx_dim), BF16),
        jax.ShapeDtypeStruct((SUBLANES, T), F32),
        jax.ShapeDtypeStruct((T, n_u), F32),
        jax.ShapeDtypeStruct((T, n_gate), BF16),
    )
    in_specs = [
        pl.BlockSpec((tm, D), lambda i: (i, 0)),
        _const_spec((1, D)), _const_spec(wq.shape), _const_spec(wuk_pairs.shape), _const_spec(wm.shape),
        _const_spec((1, R)), _const_spec(wwi.shape),
        _const_spec(wu.shape), _const_spec(wgl.shape), _const_spec((1, n_gate)),
    ]
    out_specs = (
        pl.BlockSpec((H, tm, R), lambda i: (0, i, 0)),
        pl.BlockSpec((tm, R), lambda i: (i, 0)),
        pl.BlockSpec((n_ih, tm, idx_dim), lambda i: (0, i, 0)),
        pl.BlockSpec((tm, idx_dim), lambda i: (i, 0)),
        pl.BlockSpec((SUBLANES, tm), lambda i: (0, i)),
        pl.BlockSpec((tm, n_u), lambda i: (i, 0)),
        pl.BlockSpec((tm, n_gate), lambda i: (i, 0)),
    )
    return pl.pallas_call(
        functools.partial(_inproj_kernel, q_scale=dh ** -0.5 * LOG2E),
        grid=(T // tm,), in_specs=in_specs, out_specs=out_specs, out_shape=out_shape,
        compiler_params=_params(("parallel",)), name="inproj",
    )(x2, g_mix.reshape(1, D), wq, wuk_pairs, wm, g_kv.reshape(1, R), wwi, wu, wgl,
      b_gate.reshape(1, n_gate))


def _attn_kernel(qlat_ref, qi_ref, wit_ref, ki_ref, c_ref, ct_ref, bias_ref, tri_ref, eye_ref, wuv_ref, o_ref,
                 keys_ref, hi_ref, lo_ref, mask_ref, qa_ref, lga_ref, lgb_ref, tma_ref, tmb_ref, m_ref, acc_ref,
                 tie_ref, *, top_k):
    H, TQ, R = qlat_ref.shape
    n_ih = qi_ref.shape[0]
    TK = tri_ref.shape[0]
    j = pl.program_id(1)
    q0 = j * TQ
    n_kt = (q0 + TQ + TK - 1) // TK
    n_pair = (n_kt + 1) // 2
    n_quad = n_pair // 2
    HT = TK // 2

    lane = lax.broadcasted_iota(I32, (1, TQ), 1)
    limit = ((q0 + lane) // CHUNK + 1) * CHUNK

    qi_all = qi_ref[...].reshape(n_ih * TQ, qi_ref.shape[2])
    wt = wit_ref[...]

    def score_tile(kt):
        k0 = pl.multiple_of(kt * TK, TK)
        sc = _dot_nt(ki_ref[pl.ds(k0, TK), :], qi_all)
        s = jnp.zeros((TK, TQ), F32)
        for h in range(n_ih):
            s = s + wt[h:h + 1, :] * jnp.maximum(sc[:, h * TQ:(h + 1) * TQ], 0.0)
        s = jnp.where(s == 0.0, 0.0, s)
        bits = lax.bitcast_convert_type(s, I32)
        key = bits ^ ((bits >> 31) & 0x7FFFFFFF)
        row = k0 + lax.broadcasted_iota(I32, (TK, 1), 0)
        key = jnp.where(row < limit, key, INT_MIN)
        keys_ref[pl.ds(k0, TK), :] = key
        hi_ref[pl.ds(pl.multiple_of(kt * HT, HT), HT), :] = (key[:HT] & -65536) | ((key[HT:] >> 16) & 0xFFFF)

    def score_quad(i, carry):
        for t in range(4):
            score_tile(4 * i + t)
        return carry

    def score_rest(i, carry):
        score_tile(2 * i)
        score_tile(2 * i + 1)
        return carry

    lax.fori_loop(0, n_quad, score_quad, 0)
    lax.fori_loop(2 * n_quad, n_pair, score_rest, 0)

    one16, zero16 = jnp.int16(1), jnp.int16(0)

    def count16(ref, pred):
        def step(pair, n_pairs, acc):
            words = ref[pl.ds(pl.multiple_of(pair * TK, TK), n_pairs * TK), :]
            ind = jnp.where(pred(pltpu.bitcast(words, jnp.int16)), one16, zero16)
            parts = [ind[PACK * r:PACK * (r + 1)] for r in range(n_pairs * 2 * TK // PACK)]
            while len(parts) > 1:
                parts = [a + b for a, b in zip(parts[::2], parts[1::2])]
            return acc + parts[0]
        acc = lax.fori_loop(0, n_quad, lambda i, a: step(2 * i, 2, a), jnp.zeros((PACK, TQ), jnp.int16))
        acc = lax.fori_loop(2 * n_quad, n_pair, lambda i, a: step(i, 1, a), acc).astype(I32)
        return (acc[:SUBLANES] + acc[SUBLANES:]).sum(axis=0, keepdims=True)

    def kth_largest16(ref, k):
        def bit_body(i, lo):
            cand = lo + lax.shift_left(jnp.int32(1), 15 - i)
            c16 = cand.astype(jnp.int16)
            cnt = count16(ref, lambda blk: blk >= c16)
            return jnp.where(cnt >= k, cand, lo)
        return lax.fori_loop(0, 16, bit_body, jnp.full((1, TQ), INT16_MIN, I32))

    t_hi = kth_largest16(hi_ref, top_k)
    t_hi16 = t_hi.astype(jnp.int16)
    n_hi_gt = count16(hi_ref, lambda blk: blk > t_hi16)

    def low_tile(kt, carry):
        key = keys_ref[pl.ds(pl.multiple_of(kt * TK, TK), TK), :]
        low = jnp.where((key >> 16) == t_hi, (key ^ 0x8000) & 0xFFFF, 0x8000)
        lo_ref[pl.ds(pl.multiple_of(kt * HT, HT), HT), :] = (low[:HT] << 16) | low[HT:]
        return carry

    lax.fori_loop(0, 2 * n_pair, low_tile, 0)
    t_lo = kth_largest16(lo_ref, top_k - n_hi_gt)
    t_lo16 = t_lo.astype(jnp.int16)
    thr = (t_hi << 16) | ((t_lo ^ 0x8000) & 0xFFFF)
    n_gt = n_hi_gt + count16(lo_ref, lambda blk: blk > t_lo16)
    n_tie = jnp.where(thr == INT_MIN, 0, top_k - n_gt).astype(F32)

    tie_ref[...] = jnp.zeros(tie_ref.shape, F32)

    def mask_tiles(first, n_tiles):
        ks = [pl.multiple_of((first + t) * TK, TK) for t in range(n_tiles)]
        keys = [keys_ref[pl.ds(k0, TK), :] for k0 in ks]
        eqs = [key == thr for key in keys]
        ranks = [_dot(tri_ref[...], jnp.where(eq, 1.0, 0.0).astype(BF16)) for eq in eqs]
        tie = tie_ref[...]
        for k0, key, eq, rank in zip(ks, keys, eqs, ranks):
            rank = rank + tie
            tie = rank[TK - 1:TK, :]
            negmask = jnp.where(key > thr, 0.0,
                                jnp.where(eq, jnp.where(rank <= n_tie, 0.0, NEG_MASK), NEG_MASK))
            mask_ref[pl.ds(k0, TK), :] = negmask.astype(BF16)
        tie_ref[...] = tie

    def mask_quad(i, carry):
        mask_tiles(4 * i, 4)
        return carry

    def mask_rest(i, carry):
        mask_tiles(2 * i, 2)
        return carry

    lax.fori_loop(0, n_quad, mask_quad, 0)
    lax.fori_loop(2 * n_quad, n_pair, mask_rest, 0)

    qa_ref[...] = jnp.concatenate([qlat_ref[...].reshape(H * TQ, R), eye_ref[...]], axis=1)
    m_ref[...] = jnp.full(m_ref.shape, M_FLOOR, F32)
    acc_ref[...] = jnp.zeros(acc_ref.shape, F32)

    def issue_logits(kt, lg_ref, tm_ref):
        k0 = pl.multiple_of(jnp.minimum(kt, 2 * n_pair - 1) * TK, TK)
        c_aug = jnp.concatenate([c_ref[pl.ds(k0, TK), :], mask_ref[pl.ds(k0, TK), :]], axis=1)
        logits = _dot_nt(c_aug, qa_ref[...])
        boff = pl.multiple_of(jnp.clip(k0 - q0 + 2 * TK, 0, 2 * TK), LANES)
        for h in range(H):
            sl = slice(h * TQ, (h + 1) * TQ)
            lg = logits[:, sl] + bias_ref[h, pl.ds(boff, TK), :]
            lg_ref[:, sl] = lg
            tm_ref[:, sl] = jnp.max(lg, axis=0, keepdims=True)

    def softmax_pv(kt, lg_ref, tm_ref):
        ct_tile = ct_ref[kt]
        for hp in range(H // 2):
            cols = slice(2 * hp * TQ, 2 * (hp + 1) * TQ)
            m_old = m_ref[:, cols]
            m_new = jnp.maximum(m_old, tm_ref[:, cols])
            m_ref[:, cols] = m_new
            alpha = jnp.exp2(m_old - m_new)
            p = jnp.exp2(lg_ref[:, cols] - m_new).astype(BF16)
            acc_ref[:, cols] = acc_ref[:, cols] * alpha + _dot(ct_tile, p)

    issue_logits(0, lga_ref, tma_ref)

    def flash_pair(i):
        issue_logits(2 * i + 1, lgb_ref, tmb_ref)
        softmax_pv(2 * i, lga_ref, tma_ref)
        issue_logits(2 * i + 2, lga_ref, tma_ref)
        softmax_pv(2 * i + 1, lgb_ref, tmb_ref)

    def flash_quad(i, carry):
        flash_pair(2 * i)
        flash_pair(2 * i + 1)
        return carry

    def flash_rest(i, carry):
        flash_pair(i)
        return carry

    lax.fori_loop(0, n_quad, flash_quad, 0)
    lax.fori_loop(2 * n_quad, n_pair, flash_rest, 0)

    olat = acc_ref[:R, :] * (1.0 / acc_ref[R:R + 1, :])
    o = jnp.zeros((TQ, wuv_ref.shape[2]), F32)
    for h in range(H):
        o = o + _dot(olat[:, h * TQ:(h + 1) * TQ].T.astype(BF16), wuv_ref[h])
    o_ref[...] = o.astype(BF16)


def _t5_bucket(rel):
    half = N_BUCKETS // 2
    max_exact = half // 2
    ret = jnp.where(rel > 0, half, 0)
    n = jnp.abs(rel)
    nf = jnp.maximum(n, 1).astype(F32)
    large = max_exact + (jnp.log(nf / max_exact) / math.log(MAX_DISTANCE / max_exact)
                         * (half - max_exact)).astype(I32)
    large = jnp.minimum(large, half - 1)
    return ret + jnp.where(n < max_exact, n, large)


def _attention(qlat, c, qi, ki, wit, w_uv, rel_bias, B, S):
    H, T, R = qlat.shape
    n_ih, _, idx_dim = qi.shape
    dh = w_uv.shape[2]
    TQ, TK = ATTN_TQ, ATTN_TK
    assert S % (2 * TK) == 0 and S % TQ == 0 and TK % TQ == 0 and MAX_DISTANCE <= TQ
    top_k = min(TOPK_MAX, S // 4)
    n_qt = S // TQ
    i = jnp.arange(3 * TK, dtype=I32)[:, None] - 2 * TK
    jq = jnp.arange(TQ, dtype=I32)[None, :]
    bucket = _t5_bucket(i - jq)[None]
    rb = (rel_bias - rel_bias[_t5_bucket(jnp.int32(-MAX_DISTANCE))][None, :]).astype(F32).T
    bias_t = jnp.zeros((H,) + bucket.shape[1:], F32)
    for k in range(N_BUCKETS):
        bias_t = jnp.where(bucket == k, rb[:, k][:, None, None], bias_t)
    bias_t = bias_t * LOG2E
    tri = (jnp.arange(TK)[:, None] >= jnp.arange(TK)[None, :]).astype(BF16)
    eye = jnp.tile(jnp.eye(TQ, dtype=BF16), (H, 1))
    wuv_h = jnp.transpose(w_uv, (1, 0, 2)).astype(BF16)
    wuv_pad = jnp.zeros((H, R, H * dh), BF16)
    for h in range(H):
        wuv_pad = wuv_pad.at[h, :, h * dh:(h + 1) * dh].set(wuv_h[h])
    c3 = c.reshape(B, S, R)
    ct = jnp.transpose(c3.reshape(B, S // TK, TK, R), (0, 1, 3, 2))
    ones_rows = jnp.zeros((B, S // TK, ONES_ROWS, TK), BF16).at[:, :, 0, :].set(1.0)
    ct = jnp.concatenate([ct, ones_rows], axis=2)
    grid = (B, n_qt)
    in_specs = [
        pl.BlockSpec((H, None, TQ, R), lambda b, j: (0, b, j, 0)),
        pl.BlockSpec((n_ih, None, TQ, idx_dim), lambda b, j: (0, b, j, 0)),
        pl.BlockSpec((SUBLANES, TQ), lambda b, j: (0, b * n_qt + j)),
        pl.BlockSpec((None, S, idx_dim), lambda b, j: (b, 0, 0)),
        pl.BlockSpec((None, S, R), lambda b, j: (b, 0, 0)),
        pl.BlockSpec((None, S // TK, R + ONES_ROWS, TK), lambda b, j: (b, 0, 0, 0)),
        _const_spec(bias_t.shape), _const_spec(tri.shape), _const_spec(eye.shape), _const_spec(wuv_pad.shape),
    ]
    return pl.pallas_call(
        functools.partial(_attn_kernel, top_k=top_k),
        grid=grid, in_specs=in_specs,
        out_specs=pl.BlockSpec((None, TQ, H * dh), lambda b, j: (b, j, 0)),
        out_shape=jax.ShapeDtypeStruct((B, S, H * dh), BF16),
        scratch_shapes=[pltpu.VMEM((S, TQ), I32), pltpu.VMEM((S // 2, TQ), I32), pltpu.VMEM((S // 2, TQ), I32),
                        pltpu.VMEM((S, TQ), BF16),
                        pltpu.VMEM((H * TQ, R + TQ), BF16), pltpu.VMEM((TK, H * TQ), F32),
                        pltpu.VMEM((TK, H * TQ), F32), pltpu.VMEM((1, H * TQ), F32), pltpu.VMEM((1, H * TQ), F32),
                        pltpu.VMEM((1, H * TQ), F32),
                        pltpu.VMEM((R + ONES_ROWS, H * TQ), F32), pltpu.VMEM((1, TQ), F32)],
        compiler_params=_params(("parallel", "arbitrary")), name="dsa_attention",
    )(qlat.reshape(H, B, S, R), qi.reshape(n_ih, B, S, idx_dim), wit, ki.reshape(B, S, idx_dim), c3, ct,
      bias_t, tri, eye, wuv_pad)


def _s5_kernel(u_ref, ar_ref, ai_ref, bbr_ref, bbi_ref, ccr_ref, cci_ref, d_ref, wglu_ref, o_ref,
               xr_ref, xi_ref, sr_ref, si_ref):
    batch, steps, width = u_ref.shape
    rows = batch * steps
    n_state = ar_ref.shape[1]
    n_half = bbr_ref.shape[0]
    uw = width // n_half
    sw = n_state // n_half

    @pl.when(pl.program_id(0) == 0)
    def _():
        sr_ref[...] = jnp.zeros(sr_ref.shape, F32)
        si_ref[...] = jnp.zeros(si_ref.shape, F32)

    u = jnp.swapaxes(u_ref[...], 0, 1).reshape(rows, width)
    ub = u.astype(BF16)
    for j in range(n_half):
        uj = ub[:, uw * j:uw * (j + 1)]
        xr_ref[:, sw * j:sw * (j + 1)] = _dot(uj, bbr_ref[j])
        xi_ref[:, sw * j:sw * (j + 1)] = _dot(uj, bbi_ref[j])

    cb = S5_COL_BLOCK
    for k in range(n_state // cb):
        cs = slice(cb * k, cb * (k + 1))
        ar = jnp.broadcast_to(ar_ref[:, cs], (batch, cb))
        ai = jnp.broadcast_to(ai_ref[:, cs], (batch, cb))

        def step(t, carry):
            xr, xi = carry
            r0 = pl.multiple_of(t * batch, batch)
            nr = ar * xr - ai * xi + xr_ref[pl.ds(r0, batch), cs]
            ni = ar * xi + ai * xr + xi_ref[pl.ds(r0, batch), cs]
            xr_ref[pl.ds(r0, batch), cs] = nr
            xi_ref[pl.ds(r0, batch), cs] = ni
            return nr, ni

        xr, xi = lax.fori_loop(0, steps, step, (sr_ref[:, cs], si_ref[:, cs]), unroll=8)
        sr_ref[:, cs] = xr
        si_ref[:, cs] = xi

    ys = []
    for j in range(n_half):
        xs_r = xr_ref[:, sw * j:sw * (j + 1)].astype(BF16)
        xs_i = xi_ref[:, sw * j:sw * (j + 1)].astype(BF16)
        ys.append(_dot(xs_r, ccr_ref[j]) - _dot(xs_i, cci_ref[j]))
    y = jnp.concatenate(ys, axis=1) + d_ref[...] * u
    z = 0.5 * y * (1.0 + jnp.tanh(math.sqrt(2.0 / math.pi) * (y + 0.044715 * (y * y * y))))
    vg = _dot(z.astype(BF16), wglu_ref[...])
    half = vg.shape[1] // 2
    out = vg[:, :half] * (1.0 / (1.0 + jnp.exp(-vg[:, half:])))
    o_ref[...] = jnp.swapaxes(out.reshape(steps, batch, half), 0, 1).astype(BF16)


def _block_diag_halves(blocks, n_half):
    G, r, c = blocks.shape
    g2 = G // n_half
    eye = jnp.eye(g2, dtype=blocks.dtype)
    b = blocks.reshape(n_half, g2, r, c)
    return jnp.einsum('jgrc,gh->jgrhc', b, eye).reshape(n_half, g2 * r, g2 * c)


def _s5(u3, a_re, a_im, log_dt, b_re, b_im, c_re, c_im, d_skip, w_glu):
    batch, seq, width = u3.shape
    G, N = a_re.shape
    P = b_re.shape[2]
    n_state = G * N
    n_half = 2
    dt = jnp.exp(log_dt.astype(F32))[:, None]
    ar, ai = a_re.astype(F32), a_im.astype(F32)
    mag = jnp.exp(ar * dt)
    abar_r = mag * jnp.cos(ai * dt)
    abar_i = mag * jnp.sin(ai * dt)
    nr, ni = abar_r - 1.0, abar_i
    den = ar * ar + ai * ai
    fr = (nr * ar + ni * ai) / den
    fi = (ni * ar - nr * ai) / den
    br_, bi_ = b_re.astype(F32), b_im.astype(F32)
    bb_r = fr[..., None] * br_ - fi[..., None] * bi_
    bb_i = fr[..., None] * bi_ + fi[..., None] * br_
    bbr = _block_diag_halves(jnp.transpose(bb_r, (0, 2, 1)), n_half).astype(BF16)
    bbi = _block_diag_halves(jnp.transpose(bb_i, (0, 2, 1)), n_half).astype(BF16)
    ccr = _block_diag_halves(jnp.transpose(c_re.astype(F32), (0, 2, 1)), n_half).astype(BF16)
    cci = _block_diag_halves(jnp.transpose(c_im.astype(F32), (0, 2, 1)), n_half).astype(BF16)
    tc = S5_TIME_CHUNK
    rows = tc * batch
    n_out = w_glu.shape[1] // 2
    assert batch % SUBLANES == 0 and seq % tc == 0 and n_state % S5_COL_BLOCK == 0
    in_specs = [
        pl.BlockSpec((batch, tc, width), lambda i: (0, i, 0)),
        _const_spec((1, n_state)), _const_spec((1, n_state)),
        _const_spec(bbr.shape), _const_spec(bbi.shape), _const_spec(ccr.shape), _const_spec(cci.shape),
        _const_spec((1, width)), _const_spec(w_glu.shape),
    ]
    return pl.pallas_call(
        _s5_kernel, grid=(seq // tc,), in_specs=in_specs,
        out_specs=pl.BlockSpec((batch, tc, n_out), lambda i: (0, i, 0)),
        out_shape=jax.ShapeDtypeStruct((batch, seq, n_out), BF16),
        scratch_shapes=[pltpu.VMEM((rows, n_state), F32), pltpu.VMEM((rows, n_state), F32),
                        pltpu.VMEM((batch, n_state), F32), pltpu.VMEM((batch, n_state), F32)],
        compiler_params=_params(("arbitrary",)), name="s5_mixer",
    )(u3, abar_r.reshape(1, n_state), abar_i.reshape(1, n_state), bbr, bbi, ccr, cci,
      d_skip.astype(F32).reshape(1, width), w_glu.astype(BF16))


def _mix_kernel(o_ref, s_ref, gate_ref, x_ref, wa_ref, wb_ref, wo_ref, gf_ref, wr_ref, br_ref,
                x1_ref, h2_ref, route_ref, *, n_groups, per_group):
    d = x_ref.shape[1]
    tm = x_ref.shape[0]
    ya = _dot(o_ref[...], wa_ref[...])
    yb = _dot(s_ref[...], wb_ref[...])
    mix = gate_ref[:, :d].astype(F32) * ya + gate_ref[:, d:].astype(F32) * yb
    x1 = x_ref[...] + _dot(mix.astype(BF16), wo_ref[...])
    x1_ref[...] = x1
    h2b = _rms(x1, gf_ref[...]).astype(BF16)
    h2_ref[...] = h2b
    lt = _dot_nt(wr_ref[...], h2b) + br_ref[...]
    n_exp = n_groups * per_group
    lg = lt[n_exp:n_exp + n_groups, :]
    eg = jnp.exp(lg - jnp.max(lg, axis=0, keepdims=True))
    pg = eg / jnp.sum(eg, axis=0, keepdims=True)
    gval = jnp.max(pg, axis=0, keepdims=True)
    gi = lax.broadcasted_iota(I32, (n_groups, tm), 0)
    grp = jnp.min(jnp.where(pg == gval, gi, n_groups), axis=0, keepdims=True)
    le = jnp.zeros((per_group, tm), F32)
    for g in range(n_groups):
        le = jnp.where(grp == g, lt[per_group * g:per_group * (g + 1), :], le)
    ee = jnp.exp(le - jnp.max(le, axis=0, keepdims=True))
    pe = ee / jnp.sum(ee, axis=0, keepdims=True)
    ei = lax.broadcasted_iota(I32, (per_group, tm), 0)
    p0 = jnp.max(pe, axis=0, keepdims=True)
    i0 = jnp.min(jnp.where(pe == p0, ei, per_group), axis=0, keepdims=True)
    pe2 = jnp.where(ei == i0, -1.0, pe)
    p1 = jnp.max(pe2, axis=0, keepdims=True)
    i1 = jnp.min(jnp.where(pe2 == p1, ei, per_group), axis=0, keepdims=True)
    psum = p0 + p1
    w0 = gval * (p0 / psum)
    w1 = gval * (p1 / psum)
    e0 = (grp * per_group + i0).astype(F32)
    e1 = (grp * per_group + i1).astype(F32)
    zero = jnp.zeros((SUBLANES - 4, tm), F32)
    route_ref[...] = jnp.concatenate([e0, e1, w0, w1, zero], axis=0)


def _mix(o, s5g, gates, x2, w_up_a, w_up_b, w_out, g_ffn, w_rg, b_rg, w_re, b_re):
    T, D = x2.shape
    tm = ROW_TILE
    n_groups = w_rg.shape[1]
    n_exp = w_re.shape[1]
    per_group = n_exp // n_groups
    n_r = -(-(n_exp + n_groups) // SUBLANES) * SUBLANES
    wr = jnp.zeros((n_r, D), BF16).at[:n_exp].set(w_re.T.astype(BF16)).at[n_exp:n_exp + n_groups].set(
        w_rg.T.astype(BF16))
    br = jnp.zeros((n_r, 1), F32).at[:n_exp, 0].set(b_re.astype(F32)).at[n_exp:n_exp + n_groups, 0].set(
        b_rg.astype(F32))
    wa, wb_, wo = w_up_a.astype(BF16), w_up_b.astype(BF16), w_out.astype(BF16)
    in_specs = [
        pl.BlockSpec((tm, o.shape[1]), lambda i: (i, 0)),
        pl.BlockSpec((tm, s5g.shape[1]), lambda i: (i, 0)),
        pl.BlockSpec((tm, gates.shape[1]), lambda i: (i, 0)),
        pl.BlockSpec((tm, D), lambda i: (i, 0)),
        _const_spec(wa.shape), _const_spec(wb_.shape), _const_spec(wo.shape), _const_spec((1, D)),
        _const_spec(wr.shape), _const_spec(br.shape),
    ]
    out_shape = (jax.ShapeDtypeStruct((T, D), F32), jax.ShapeDtypeStruct((T, D), BF16),
                 jax.ShapeDtypeStruct((SUBLANES, T), F32))
    out_specs = (pl.BlockSpec((tm, D), lambda i: (i, 0)), pl.BlockSpec((tm, D), lambda i: (i, 0)),
                 pl.BlockSpec((SUBLANES, tm), lambda i: (0, i)))
    return pl.pallas_call(
        functools.partial(_mix_kernel, n_groups=n_groups, per_group=per_group),
        grid=(T // tm,), in_specs=in_specs, out_specs=out_specs, out_shape=out_shape,
        compiler_params=_params(("parallel",)), name="mix_router",
    )(o, s5g, gates, x2, wa, wb_, wo, g_ffn.reshape(1, D), wr, br)


def _moe_kernel(be_ref, x_ref, rw_ref, wg_ref, wu_ref, wd_ref, y_ref):
    del be_ref
    xb = x_ref[...]
    g = _dot(xb, wg_ref[...])
    hb = (g * (1.0 / (1.0 + jnp.exp(-g)))) * _dot(xb, wu_ref[...])
    rw = jnp.transpose(rw_ref[...])[:, 0:1]
    y_ref[...] = (_dot(hb.astype(BF16), wd_ref[...]) * rw).astype(BF16)


def _moe_ffn(xs, row_w, blk_exp, w_gate, w_up, w_down):
    n_rows, D = xs.shape
    E, _, De = w_gate.shape
    bm = MOE_ROWS
    rw3 = jnp.pad(row_w.reshape(n_rows // bm, 1, bm), ((0, 0), (0, SUBLANES - 1), (0, 0)))
    grid_spec = pltpu.PrefetchScalarGridSpec(
        num_scalar_prefetch=1, grid=(n_rows // bm,),
        in_specs=[
            pl.BlockSpec((bm, D), lambda i, be: (i, 0)),
            pl.BlockSpec((None, SUBLANES, bm), lambda i, be: (i, 0, 0)),
            pl.BlockSpec((None, D, De), lambda i, be: (be[i], 0, 0)),
            pl.BlockSpec((None, D, De), lambda i, be: (be[i], 0, 0)),
            pl.BlockSpec((None, De, D), lambda i, be: (be[i], 0, 0)),
        ],
        out_specs=pl.BlockSpec((bm, D), lambda i, be: (i, 0)),
    )
    return pl.pallas_call(
        _moe_kernel, grid_spec=grid_spec, out_shape=jax.ShapeDtypeStruct((n_rows, D), BF16),
        compiler_params=_params(("arbitrary",)), name="moe_ffn",
    )(blk_exp, xs, rw3, w_gate.astype(BF16), w_up.astype(BF16), w_down.astype(BF16))


def _final_kernel(x1_ref, y0_ref, y1_ref, g_ref, o_ref):
    o_ref[...] = _rms(x1_ref[...] + (y0_ref[...].astype(F32) + y1_ref[...].astype(F32)), g_ref[...])


def _final(x1, y0, y1, g_final):
    T, D = x1.shape
    tm = ROW_TILE
    spec = pl.BlockSpec((tm, D), lambda i: (i, 0))
    return pl.pallas_call(
        _final_kernel, grid=(T // tm,), in_specs=[spec, spec, spec, _const_spec((1, D))], out_specs=spec,
        out_shape=jax.ShapeDtypeStruct((T, D), F32), compiler_params=_params(("parallel",)), name="final_norm",
    )(x1, y0, y1, g_final.reshape(1, D))


def _dispatch_plan(eid, weight, n_experts, bm):
    T = eid.shape[0]
    M = 2 * T
    e_flat = eid.reshape(M)
    a_idx = jnp.arange(M, dtype=I32)
    _, order = lax.sort((e_flat, a_idx), num_keys=1)
    _, rank = lax.sort((order, a_idx), num_keys=1)
    experts = jnp.arange(n_experts, dtype=I32)
    counts = jnp.sum((e_flat[:, None] == experts[None, :]).astype(I32), axis=0)
    starts = jnp.cumsum(counts) - counts
    padded = (counts + bm - 1) // bm * bm
    pad_end = jnp.cumsum(padded)
    pad_start = pad_end - padded
    n_rows = (M + n_experts * (bm - 1) + bm - 1) // bm * bm
    blk_start = jnp.arange(n_rows // bm, dtype=I32) * bm
    blk_exp = jnp.minimum(jnp.sum((pad_end[None, :] <= blk_start[:, None]).astype(I32), axis=1), n_experts - 1)
    off = (blk_start - pad_start[blk_exp])[:, None] + jnp.arange(bm, dtype=I32)[None, :]
    valid = (off < counts[blk_exp][:, None]).reshape(n_rows)
    src = jnp.clip(starts[blk_exp][:, None] + off, 0, M - 1).reshape(n_rows)
    a_row = order[src]
    row_tok = jnp.where(valid, a_row // 2, jnp.arange(n_rows, dtype=I32) % T)
    row_w = jnp.where(valid, weight.reshape(M)[a_row], 0.0)
    pos = pad_start[e_flat] + (rank - starts[e_flat])
    return row_tok, row_w, pos.reshape(T, 2), blk_exp


def kernel(x, g_mix, w_in, g_kv, w_uk, w_uv, rel_bias, s5_a_re, s5_a_im, s5_log_dt, s5_b_re, s5_b_im,
           s5_c_re, s5_c_im, s5_d, w_glu, w_up_a, w_up_b, b_gate, w_out, g_ffn, w_router_g, b_router_g,
           w_router_e, b_router_e, w_gate, w_up, w_down, g_final):
    B, S, D = x.shape
    T = B * S
    assert S % ROW_TILE == 0 and g_mix.shape[0] == 1
    x2 = x.reshape(T, D)
    for l in range(g_mix.shape[0]):
        qlat, c, qi, ki, wit, u, gates = _inproj(x2, g_mix[l], w_in[l], g_kv[l], w_uk[l], b_gate[l])
        o = _attention(qlat, c, qi, ki, wit, w_uv[l], rel_bias, B, S).reshape(T, -1)
        s5g = _s5(u.reshape(B, S, -1), s5_a_re[l], s5_a_im[l], s5_log_dt[l], s5_b_re[l], s5_b_im[l],
                  s5_c_re[l], s5_c_im[l], s5_d[l], w_glu[l]).reshape(T, -1)
        x1, h2, route = _mix(o, s5g, gates, x2, w_up_a[l], w_up_b[l], w_out[l], g_ffn[l],
                             w_router_g[l], b_router_g[l], w_router_e[l], b_router_e[l])
        eid = jnp.transpose(route[0:2]).astype(I32)
        weight = jnp.transpose(route[2:4])
        n_experts = w_gate.shape[1]
        row_tok, row_w, pos, blk_exp = _dispatch_plan(eid, weight, n_experts, MOE_ROWS)
        yb = _moe_ffn(h2[row_tok], row_w, blk_exp, w_gate[l], w_up[l], w_down[l])
        x2 = x1
        y0, y1 = yb[pos[:, 0]], yb[pos[:, 1]]
    out = _final(x2, y0, y1, g_final)
    return out.reshape(B, S, D)
```
